```python
import math
import jax
import jax.numpy as jnp
from jax import lax
import numpy as np

D_MODEL = 1024
BATCH = 16
SEQ = 2048
DEPTH = 4

PLE_DIM = 256
QBLK = 128
NEG_INF = -1e30
EPS = 1e-6

DA_DH = 64
DA_HEADS = (D_MODEL // 2) // (2 * DA_DH)
NS_DH = 64
NS_HEADS = (D_MODEL // 2) // NS_DH
NS_KV = 2
CMP_LEN = 32
CMP_STRIDE = 16
CMP_HID = 256
SEL_LEN = 64
SEL_TOPN = 16
WINDOW = 512
FORCE_BONUS = 1e4
RET_DK = 256
RET_HEADS = D_MODEL // RET_DK
RET_DV = 2 * RET_DK
RET_CHUNK = 128
N_EXPERTS = 32
TOP_K = 4
D_FF = D_MODEL
SWIGLU_ALPHA = 1.702
SWIGLU_LIMIT = 7.0
MOE_BLK = 128

N_EVEN = (DEPTH + 1) // 2
N_ODD = DEPTH // 2
EV_SPLITS = [DA_HEADS * DA_DH] * 4 + [DA_HEADS * 2 * DA_DH, NS_HEADS * NS_DH] + [NS_KV * NS_DH] * 6 + [NS_HEADS * 3]
EV_IN = sum(EV_SPLITS)
EV_OUT = DA_HEADS * 2 * DA_DH + NS_HEADS * NS_DH
OD_SPLITS = [RET_HEADS * RET_DK, RET_HEADS * RET_DK, RET_HEADS * RET_DV, RET_HEADS * RET_DV]
OD_IN = sum(OD_SPLITS)
OD_OUT = RET_HEADS * RET_DV

kernel_name = 'hybrid_diffattn_nsa_retnet_moe'


def rms_norm(x, g=None):
    xf = x.astype(jnp.float32)
    y = xf * lax.rsqrt(jnp.mean(xf * xf, axis=-1, keepdims=True) + EPS)
    if g is not None:
        y = y * g.astype(jnp.float32)
    return y.astype(x.dtype)


def split_cols(z, sizes):
    return jnp.split(z, np.cumsum(sizes)[:-1].tolist(), axis=-1)


def alibi_slopes(n):
    return jnp.asarray([2.0 ** (-8.0 * (h + 1) / n) for h in range(n)], jnp.float32)


def diff_attention(q1, q2, k1, k2, v, slopes, lam):
    B, S, H, dh = q1.shape
    scale = dh ** -0.5
    kpos = jnp.arange(S)

    def one_block(qb):
        q0 = qb * QBLK
        a1 = lax.dynamic_slice_in_dim(q1, q0, QBLK, 1)
        a2 = lax.dynamic_slice_in_dim(q2, q0, QBLK, 1)
        dist = (q0 + jnp.arange(QBLK))[:, None] - kpos[None, :]
        bias = jnp.where(dist >= 0, -slopes[:, None, None] * dist.astype(jnp.float32), NEG_INF)
        s1 = jnp.einsum('bqhd,bkhd->bhqk', a1, k1, preferred_element_type=jnp.float32) * scale + bias
        s2 = jnp.einsum('bqhd,bkhd->bhqk', a2, k2, preferred_element_type=jnp.float32) * scale + bias
        pdiff = jax.nn.softmax(s1, axis=-1) - lam * jax.nn.softmax(s2, axis=-1)
        return jnp.einsum('bhqk,bkhe->bqhe', pdiff.astype(v.dtype), v)

    out = lax.map(one_block, jnp.arange(S // QBLK))
    return jnp.moveaxis(out, 0, 1).reshape(B, S, H, v.shape[-1])


def nsa_attention(q, kc, vc, ks, vs, kw, vw, gates, slopes, kn_g, cmp_pe, cmp_w1, cmp_w2):
    B, S, H, dh = q.shape
    G = kc.shape[2]
    R = H // G
    f32 = jnp.float32
    scale = dh ** -0.5
    pos = jnp.arange(S)
    qg = q.reshape(B, S, G, R, dh)
    slope_gr = slopes.reshape(G, R)

    nc = (S - CMP_LEN) // CMP_STRIDE + 1
    cstart = jnp.arange(nc) * CMP_STRIDE
    cidx = cstart[:, None] + jnp.arange(CMP_LEN)[None, :]

    def compress(t, pe, w1, w2):
        blk = t[:, cidx] + pe[None, None, :, None, :]
        blk = jnp.moveaxis(blk, 3, 2).reshape(B, nc, G, CMP_LEN * dh)
        return jax.nn.silu(blk @ w1) @ w2

    k_cmp = rms_norm(compress(kc, cmp_pe[0], cmp_w1[0], cmp_w2[0]), kn_g)
    v_cmp = compress(vc, cmp_pe[1], cmp_w1[1], cmp_w2[1])
    c_valid = (cstart + CMP_LEN - 1)[None, :] <= pos[:, None]
    s_c = jnp.einsum('bsgrd,bngd->bgrsn', qg, k_cmp, preferred_element_type=f32) * scale
    p_c = jax.nn.softmax(jnp.where(c_valid, s_c, NEG_INF), axis=-1) * c_valid.any(-1)[:, None]
    o_c = jnp.einsum('bgrsn,bngd->bsgrd', p_c.astype(vc.dtype), v_cmp)

    nb = S // SEL_LEN
    bstart = jnp.arange(nb) * SEL_LEN
    lo = jnp.maximum(cstart[:, None], bstart[None, :])
    hi = jnp.minimum(cstart[:, None] + CMP_LEN, bstart[None, :] + SEL_LEN)
    overlap = jnp.maximum(hi - lo, 0).astype(f32) / CMP_LEN
    imp = jnp.einsum('bgrsn,nj->bgsj', p_c, overlap)
    cur = pos // SEL_LEN
    jb = jnp.arange(nb)
    forced = (jb[None, :] == 0) | (jb[None, :] == cur[:, None]) | (jb[None, :] == cur[:, None] - 1)
    score = jnp.where(jb[None, :] > cur[:, None], NEG_INF, imp + FORCE_BONUS * forced)
    n_sel = min(SEL_TOPN, nb)
    _, sel_idx = lax.top_k(score, n_sel)

    ks_blk = jnp.moveaxis(ks.reshape(B, nb, SEL_LEN, G, dh), 3, 1)
    vs_blk = jnp.moveaxis(vs.reshape(B, nb, SEL_LEN, G, dh), 3, 1)
    kw_pad = jnp.pad(kw, ((0, 0), (WINDOW, 0), (0, 0), (0, 0)))
    vw_pad = jnp.pad(vw, ((0, 0), (WINDOW, 0), (0, 0), (0, 0)))
    gather = jax.vmap(jax.vmap(lambda t, i: t[i]))

    def one_block(qb):
        q0 = qb * QBLK
        tq = q0 + jnp.arange(QBLK)
        qblk = lax.dynamic_slice_in_dim(qg, q0, QBLK, 1)
        ib = lax.dynamic_slice_in_dim(sel_idx, q0, QBLK, 2)
        k_s = gather(ks_blk, ib)
        v_s = gather(vs_blk, ib)
        spos = ib[..., None] * SEL_LEN + jnp.arange(SEL_LEN)
        sdist = (tq[:, None, None] - spos).astype(f32)
        s_s = jnp.einsum('bqgrd,bgqnld->bgrqnl', qblk, k_s, preferred_element_type=f32) * scale
        s_s = jnp.where((sdist >= 0)[:, :, None],
                        s_s - slope_gr[None, :, :, None, None, None] * sdist[:, :, None], NEG_INF)
        p_s = jax.nn.softmax(s_s.reshape(B, G, R, QBLK, n_sel * SEL_LEN), axis=-1).reshape(s_s.shape)
        o_s = jnp.einsum('bgrqnl,bgqnld->bqgrd', p_s.astype(v_s.dtype), v_s)
        k_w = lax.dynamic_slice_in_dim(kw_pad, q0, WINDOW + QBLK, 1)
        v_w = lax.dynamic_slice_in_dim(vw_pad, q0, WINDOW + QBLK, 1)
        wpos = q0 - WINDOW + jnp.arange(WINDOW + QBLK)
        wdist = tq[:, None] - wpos[None, :]
        w_ok = (wdist >= 0) & (wdist < WINDOW) & (wpos >= 0)[None, :]
        s_w = jnp.einsum('bqgrd,bkgd->bgrqk', qblk, k_w, preferred_element_type=f32) * scale
        s_w = jnp.where(w_ok, s_w - slope_gr[:, :, None, None] * wdist.astype(f32), NEG_INF)
        o_w = jnp.einsum('bgrqk,bkgd->bqgrd', jax.nn.softmax(s_w, axis=-1).astype(v_w.dtype), v_w)
        return o_s, o_w

    o_s, o_w = lax.map(one_block, jnp.arange(S // QBLK))
    o_s = jnp.moveaxis(o_s, 0, 1).reshape(B, S, G, R, dh)
    o_w = jnp.moveaxis(o_w, 0, 1).reshape(B, S, G, R, dh)
    gg = gates.reshape(B, S, G, R, 3)
    o = gg[..., 0:1] * o_c + gg[..., 1:2] * o_s + gg[..., 2:3] * o_w
    return o.reshape(B, S, H * dh)


def even_mixer(xn, w_in, w_out, da_qn_g, da_kn_g, da_lam, da_out_g,
               ns_qn_g, ns_kn_g, ns_cmp_pe, ns_cmp_w1, ns_cmp_w2, layer):
    B, S, _ = xn.shape
    z = xn @ w_in
    (q1, q2, k1, k2, va, qn, kc, vc, ks, vs, kw, vw, gl) = split_cols(z, EV_SPLITS)
    hd = lambda t, n: t.reshape(B, S, n, -1)
    q1 = rms_norm(hd(q1, DA_HEADS), da_qn_g)
    q2 = rms_norm(hd(q2, DA_HEADS), da_qn_g)
    k1 = rms_norm(hd(k1, DA_HEADS), da_kn_g)
    k2 = rms_norm(hd(k2, DA_HEADS), da_kn_g)
    lam_init = 0.8 - 0.6 * math.exp(-0.3 * layer)
    lam = (jnp.exp(jnp.sum(da_lam[0] * da_lam[1]).astype(jnp.float32))
           - jnp.exp(jnp.sum(da_lam[2] * da_lam[3]).astype(jnp.float32)) + lam_init)
    oa = diff_attention(q1, q2, k1, k2, hd(va, DA_HEADS), alibi_slopes(DA_HEADS), lam)
    oa = (rms_norm(oa, da_out_g) * (1.0 - lam_init)).reshape(B, S, -1)
    ob = nsa_attention(rms_norm(hd(qn, NS_HEADS), ns_qn_g),
                       hd(kc, NS_KV), hd(vc, NS_KV),
                       rms_norm(hd(ks, NS_KV), ns_kn_g), hd(vs, NS_KV),
                       rms_norm(hd(kw, NS_KV), ns_kn_g), hd(vw, NS_KV),
                       jax.nn.sigmoid(gl).reshape(B, S, NS_HEADS, 3), alibi_slopes(NS_HEADS),
                       ns_kn_g, ns_cmp_pe, ns_cmp_w1, ns_cmp_w2)
    return jnp.concatenate([oa, ob], axis=-1) @ w_out


def retention_chunkwise(q, k, v):
    B, S, H, dk = q.shape
    dv = v.shape[-1]
    C = RET_CHUNK
    n = S // C
    f32 = jnp.float32
    lg = jnp.log(1.0 - 2.0 ** (-5.0 - jnp.arange(H, dtype=f32)))
    j = jnp.arange(C, dtype=f32)
    rel = j[:, None] - j[None, :]
    inner = jnp.where(rel >= 0, jnp.exp(lg[:, None, None] * jnp.maximum(rel, 0.0)), 0.0)
    xi = jnp.exp(lg[:, None] * (j + 1.0))[:, :, None]
    zeta = jnp.exp(lg[:, None] * (C - 1.0 - j))[:, :, None]
    decay_c = jnp.exp(lg * C)[:, None, None]
    to_chunks = lambda t: t.astype(f32).reshape(B, n, C, H, t.shape[-1]).transpose(1, 0, 3, 2, 4)

    def step(state, inp):
        qi, ki, vi = inp
        att = jnp.einsum('bhcd,bhed->bhce', qi, ki) * inner
        y = jnp.einsum('bhce,bhef->bhcf', att, vi) + jnp.einsum('bhcd,bhdf->bhcf', qi * xi, state)
        state = decay_c * state + jnp.einsum('bhcd,bhcf->bhdf', ki * zeta, vi)
        return state, y

    state0 = jnp.zeros((B, H, dk, dv), f32)
    _, ys = lax.scan(step, state0, (to_chunks(q), to_chunks(k), to_chunks(v)))
    return ys.transpose(1, 0, 3, 2, 4).reshape(B, S, H, dv).astype(v.dtype)


def retention_mixer(xn, w_in, w_out, out_g):
    B, S, _ = xn.shape
    q, k, v, g = split_cols(xn @ w_in, OD_SPLITS)
    q = q.reshape(B, S, RET_HEADS, RET_DK)
    k = k.reshape(B, S, RET_HEADS, RET_DK) * (RET_DK ** -0.5)
    v = v.reshape(B, S, RET_HEADS, RET_DV)
    y = rms_norm(retention_chunkwise(q, k, v), out_g).reshape(B, S, OD_OUT)
    return (jax.nn.silu(g) * y) @ w_out


def moe_ffn(x, router_w, router_b, w_gu, b_gu, w_down, b_down):
    B, S, D = x.shape
    T = B * S
    xt = x.reshape(T, D)
    logits = jnp.matmul(xt, router_w, preferred_element_type=jnp.float32) + router_b.astype(jnp.float32)
    top_v, top_e = lax.top_k(logits, TOP_K)
    gate = jax.nn.softmax(top_v, axis=-1)
    flat_e = top_e.reshape(-1)
    order = jnp.argsort(flat_e)
    e_sorted = flat_e[order]
    tok = (order // TOP_K).astype(jnp.int32)
    counts = jnp.bincount(flat_e, length=N_EXPERTS)
    padded = (counts + MOE_BLK - 1) // MOE_BLK * MOE_BLK
    pad_end = jnp.cumsum(padded)
    pad_start = pad_end - padded
    start = jnp.cumsum(counts) - counts
    dest = pad_start[e_sorted] + jnp.arange(T * TOP_K) - start[e_sorted]
    n_blocks = -(-(T * TOP_K) // MOE_BLK) + N_EXPERTS
    cap = n_blocks * MOE_BLK
    slot_tok = jnp.full((cap,), T, jnp.int32).at[dest].set(tok)
    x_pad = jnp.concatenate([xt, jnp.zeros((1, D), xt.dtype)], axis=0)
    xb = x_pad[slot_tok].reshape(n_blocks, MOE_BLK, D)
    blk_e = jnp.minimum(jnp.searchsorted(pad_end, jnp.arange(n_blocks) * MOE_BLK, side='right'), N_EXPERTS - 1)

    def expert_block(args):
        xblk, e = args
        h = xblk @ w_gu[e] + b_gu[e]
        hg, hl = jnp.split(h, 2, axis=-1)
        hg = jnp.minimum(hg, SWIGLU_LIMIT)
        hl = jnp.clip(hl, -SWIGLU_LIMIT, SWIGLU_LIMIT)
        a = hg * jax.nn.sigmoid(SWIGLU_ALPHA * hg) * (hl + 1.0)
        return a @ w_down[e] + b_down[e]

    yb = lax.map(expert_block, (xb, blk_e)).reshape(cap, D)
    y_rows = yb[dest] * gate.reshape(-1)[order][:, None].astype(yb.dtype)
    return jax.ops.segment_sum(y_rows, tok, num_segments=T).reshape(B, S, D)


def setup_inputs(seed: int = 0) -> dict:
    key = jax.random.key(seed)
    keys = iter(jax.random.split(key, 40))
    f32 = jnp.float32
    res_mult = (2.0 * DEPTH) ** -0.5

    def w(shape, fan_in, mult=1.0):
        return jax.random.normal(next(keys), shape, f32) * (mult * fan_in ** -0.5)

    def g(shape):
        return 1.0 + 0.05 * jax.random.normal(next(keys), shape, f32)

    def b(shape, s=0.01):
        return s * jax.random.normal(next(keys), shape, f32)

    return {
        'x': jax.random.normal(next(keys), (BATCH, SEQ, D_MODEL), f32),
        'p': jax.random.normal(next(keys), (DEPTH, BATCH, SEQ, PLE_DIM), f32),
        'norm1_g': g((DEPTH, D_MODEL)),
        'norm2_g': g((DEPTH, D_MODEL)),
        'ev_w_in': w((N_EVEN, D_MODEL, EV_IN), D_MODEL),
        'ev_w_out': w((N_EVEN, EV_OUT, D_MODEL), EV_OUT, res_mult),
        'da_qn_g': g((N_EVEN, DA_DH)),
        'da_kn_g': g((N_EVEN, DA_DH)),
        'da_lam': b((N_EVEN, 4, DA_DH), 0.1),
        'da_out_g': g((N_EVEN, 2 * DA_DH)),
        'ns_qn_g': g((N_EVEN, NS_DH)),
        'ns_kn_g': g((N_EVEN, NS_DH)),
        'ns_cmp_pe': b((N_EVEN, 2, CMP_LEN, NS_DH), 0.1),
        'ns_cmp_w1': w((N_EVEN, 2, CMP_LEN * NS_DH, CMP_HID), CMP_LEN * NS_DH),
        'ns_cmp_w2': w((N_EVEN, 2, CMP_HID, NS_DH), CMP_HID),
        'od_w_in': w((N_ODD, D_MODEL, OD_IN), D_MODEL),
        'od_w_out': w((N_ODD, OD_OUT, D_MODEL), OD_OUT, res_mult),
        'ret_out_g': g((N_ODD, RET_DV)),
        'router_w': w((DEPTH, D_MODEL, N_EXPERTS), D_MODEL),
        'router_b': b((DEPTH, N_EXPERTS), 0.01),
        'moe_w_gu': w((DEPTH, N_EXPERTS, D_MODEL, 2 * D_FF), D_MODEL),
        'moe_b_gu': b((DEPTH, N_EXPERTS, 2 * D_FF), 0.01),
        'moe_w_down': w((DEPTH, N_EXPERTS, D_FF, D_MODEL), D_FF, res_mult),
        'moe_b_down': b((DEPTH, N_EXPERTS, D_MODEL), 0.01),
        'ple_w_proj': w((DEPTH, PLE_DIM, D_MODEL), PLE_DIM, res_mult),
        'ple_w_gate': w((DEPTH, D_MODEL, D_MODEL), D_MODEL),
    }


def reference(x, p, norm1_g, norm2_g, ev_w_in, ev_w_out, da_qn_g, da_kn_g, da_lam, da_out_g,
              ns_qn_g, ns_kn_g, ns_cmp_pe, ns_cmp_w1, ns_cmp_w2, od_w_in, od_w_out, ret_out_g,
              router_w, router_b, moe_w_gu, moe_b_gu, moe_w_down, moe_b_down, ple_w_proj, ple_w_gate):
    h = x
    for i in range(DEPTH):
        hn = rms_norm(h, norm1_g[i])
        if i % 2 == 0:
            e = i // 2
            mix = even_mixer(hn, ev_w_in[e], ev_w_out[e], da_qn_g[e], da_kn_g[e], da_lam[e], da_out_g[e],
                             ns_qn_g[e], ns_kn_g[e], ns_cmp_pe[e], ns_cmp_w1[e], ns_cmp_w2[e], i)
        else:
            o = i // 2
            mix = retention_mixer(hn, od_w_in[o], od_w_out[o], ret_out_g[o])
        h = h + mix
        h = h + moe_ffn(rms_norm(h, norm2_g[i]), router_w[i], router_b[i],
                        moe_w_gu[i], moe_b_gu[i], moe_w_down[i], moe_b_down[i])
        gate = jax.nn.sigmoid(rms_norm(h) @ ple_w_gate[i])
        h = h + gate * (p[i] @ ple_w_proj[i])
    return h
```

```python
import functools
import math

import numpy as np
import jax
import jax.numpy as jnp
from jax import lax
from jax.experimental import pallas as pl
from jax.experimental.pallas import tpu as pltpu

F32 = jnp.float32
BF16 = jnp.bfloat16

D_MODEL = 1024
DEPTH = 4
PLE_DIM = 256
QBLK = 128
NEG_INF = -1e30
EPS = 1e-6
DA_DH = 64
DA_HEADS = 4
NS_DH = 64
NS_HEADS = 8
NS_KV = 2
NS_R = NS_HEADS // NS_KV
CMP_LEN = 32
CMP_STRIDE = 16
CMP_HID = 256
SEL_LEN = 64
SEL_TOPN = 16
WINDOW = 512
FORCE_BONUS = 1e4
RET_DK = 256
RET_HEADS = 4
RET_DV = 512
RET_CHUNK = 128
N_EXPERTS = 32
TOP_K = 4
D_FF = 1024
SWIGLU_ALPHA = 1.702
SWIGLU_LIMIT = 7.0

LANES = 128
VMEM_LIMIT = 56 * 1024 * 1024

Z_Q1, Z_Q2, Z_K1, Z_K2, Z_VA, Z_QN = 0, 256, 512, 768, 1024, 1536
Z_KC, Z_VC, Z_KS, Z_VS, Z_KW, Z_VW, Z_GL = 2048, 2176, 2304, 2432, 2560, 2688, 2816
EV_IN = 2840
EV_PAD = 3072
OD_IN = 6144

MOE_TM = 512
CMB_TM = 128


def _cparams(n_axes):
    return pltpu.CompilerParams(dimension_semantics=("arbitrary",) * n_axes,
                                vmem_limit_bytes=VMEM_LIMIT)


def _dot(a, b):
    return jnp.dot(a, b, preferred_element_type=F32)


def _dot_nt(a, b):
    return lax.dot_general(a, b, (((1,), (1,)), ((), ())), preferred_element_type=F32)


def _dot_tn(a, b):
    return lax.dot_general(a, b, (((0,), (0,)), ((), ())), preferred_element_type=F32)


def _split_bf16(x):
    hi = x.astype(BF16)
    lo = (x - hi.astype(F32)).astype(BF16)
    return hi, lo


def _dot_f32_by_exact(a, b_bf16):
    hi, lo = _split_bf16(a)
    return _dot(hi, b_bf16) + _dot(lo, b_bf16)


def _half_mask():
    return lax.broadcasted_iota(jnp.int32, (1, LANES), 1) >= 64


def _pair_head_norm(x, gain):
    upper = _half_mask()
    x2 = x * x
    s_lo = jnp.sum(jnp.where(upper, 0.0, x2), axis=-1, keepdims=True)
    s_hi = jnp.sum(jnp.where(upper, x2, 0.0), axis=-1, keepdims=True)
    r = jnp.where(upper, lax.rsqrt(s_hi / 64.0 + EPS), lax.rsqrt(s_lo / 64.0 + EPS))
    return x * r * gain


def _softmax_rows(s):
    m = jnp.max(s, axis=-1, keepdims=True)
    e = jnp.exp(s - m)
    return e / jnp.sum(e, axis=-1, keepdims=True)


def _norm_matmul_body(x_ref, g_ref, w_ref, o_ref, xn_ref):
    @pl.when(pl.program_id(1) == 0)
    def _():
        x = x_ref[...]
        ms = jnp.mean(x * x, axis=-1, keepdims=True)
        xn_ref[...] = (x * lax.rsqrt(ms + EPS) * g_ref[...]).astype(BF16)

    o_ref[...] = _dot(xn_ref[...], w_ref[...])


def norm_matmul(x, g, w, tm=512, tn=512):
    T, D = x.shape
    N = w.shape[1]
    return pl.pallas_call(
        _norm_matmul_body,
        grid=(T // tm, N // tn),
        in_specs=[pl.BlockSpec((tm, D), lambda i, j: (i, 0)),
                  pl.BlockSpec((1, D), lambda i, j: (0, 0)),
                  pl.BlockSpec((D, tn), lambda i, j: (0, j))],
        out_specs=pl.BlockSpec((tm, tn), lambda i, j: (i, j)),
        out_shape=jax.ShapeDtypeStruct((T, N), F32),
        scratch_shapes=[pltpu.VMEM((tm, D), BF16)],
        compiler_params=_cparams(2),
        name="norm_matmul",
    )(x, g.reshape(1, D), w)


def _out_proj2_body(h_ref, a_ref, b_ref, wa_ref, wb_ref, o_ref):
    o_ref[...] = h_ref[...] + _dot(a_ref[...], wa_ref[...]) + _dot(b_ref[...], wb_ref[...])


def out_proj2(h, a, b, wa, wb, tm=512):
    T, D = h.shape
    return pl.pallas_call(
        _out_proj2_body,
        grid=(T // tm,),
        in_specs=[pl.BlockSpec((tm, D), lambda i: (i, 0)),
                  pl.BlockSpec((tm, a.shape[1]), lambda i: (i, 0)),
                  pl.BlockSpec((tm, b.shape[1]), lambda i: (i, 0)),
                  pl.BlockSpec(wa.shape, lambda i: (0, 0)),
                  pl.BlockSpec(wb.shape, lambda i: (0, 0))],
        out_specs=pl.BlockSpec((tm, D), lambda i: (i, 0)),
        out_shape=jax.ShapeDtypeStruct((T, D), F32),
        compiler_params=_cparams(1),
        name="out_proj2",
    )(h, a, b, wa, wb)


def _out_proj1_body(h_ref, a_ref, wa_ref, o_ref):
    o_ref[...] = h_ref[...] + _dot(a_ref[...], wa_ref[...])


def out_proj1(h, a, wa, tm=512):
    T, D = h.shape
    return pl.pallas_call(
        _out_proj1_body,
        grid=(T // tm,),
        in_specs=[pl.BlockSpec((tm, D), lambda i: (i, 0)),
                  pl.BlockSpec((tm, a.shape[1]), lambda i: (i, 0)),
                  pl.BlockSpec(wa.shape, lambda i: (0, 0))],
        out_specs=pl.BlockSpec((tm, D), lambda i: (i, 0)),
        out_shape=jax.ShapeDtypeStruct((T, D), F32),
        compiler_params=_cparams(1),
        name="out_proj1",
    )(h, a, wa)


def _diff_attn_body(q1_ref, q2_ref, k1_ref, k2_ref, v_ref, qg_ref, kg_ref, lam_ref, og_ref,
                    o_ref, k1s, k2s, vs, *, S, lam_init):
    qb = pl.program_id(1)
    n_pairs = DA_HEADS // 2

    @pl.when(qb == 0)
    def _():
        for p in range(n_pairs):
            sl = slice(p * LANES, (p + 1) * LANES)
            k1s[:, sl] = _pair_head_norm(k1_ref[0, :, sl], kg_ref[...]).astype(BF16)
            k2s[:, sl] = _pair_head_norm(k2_ref[0, :, sl], kg_ref[...]).astype(BF16)
        vs[...] = v_ref[0].astype(BF16)

    lam_p = lam_ref[...]
    lam = (jnp.exp(jnp.sum(lam_p[0:1] * lam_p[1:2], axis=-1, keepdims=True))
           - jnp.exp(jnp.sum(lam_p[2:3] * lam_p[3:4], axis=-1, keepdims=True)) + lam_init)

    row = lax.broadcasted_iota(jnp.int32, (QBLK, S), 0) + qb * QBLK
    col = lax.broadcasted_iota(jnp.int32, (QBLK, S), 1)
    dist = row - col
    causal = dist >= 0
    distf = dist.astype(F32)
    upper = _half_mask()
    scale = DA_DH ** -0.5

    for p in range(n_pairs):
        sl = slice(p * LANES, (p + 1) * LANES)
        q1 = _pair_head_norm(q1_ref[0, :, sl], qg_ref[...]) * scale
        q2 = _pair_head_norm(q2_ref[0, :, sl], qg_ref[...]) * scale
        k1 = k1s[:, sl]
        k2 = k2s[:, sl]
        for j in range(2):
            h = 2 * p + j
            slope = 2.0 ** (-8.0 * (h + 1) / DA_HEADS)
            keep = upper if j == 1 else jnp.logical_not(upper)
            bias = jnp.where(causal, -slope * distf, NEG_INF)
            s1 = _dot_nt(jnp.where(keep, q1, 0.0).astype(BF16), k1) + bias
            s2 = _dot_nt(jnp.where(keep, q2, 0.0).astype(BF16), k2) + bias
            pd = _softmax_rows(s1) - lam * _softmax_rows(s2)
            o = _dot(pd.astype(BF16), vs[:, h * 2 * DA_DH:(h + 1) * 2 * DA_DH])
            ms = jnp.mean(o * o, axis=-1, keepdims=True)
            o = o * lax.rsqrt(ms + EPS) * og_ref[...] * (1.0 - lam_init)
            o_ref[0, :, h * 2 * DA_DH:(h + 1) * 2 * DA_DH] = o.astype(BF16)


def diff_attention(z, qg, kg, lam_p, og, lam_init):
    B, S, _ = z.shape
    nq = S // QBLK
    w = DA_HEADS * DA_DH
    body = functools.partial(_diff_attn_body, S=S, lam_init=lam_init)
    tile2 = lambda g: jnp.tile(g, 2).reshape(1, LANES)
    return pl.pallas_call(
        body,
        grid=(B, nq),
        in_specs=[pl.BlockSpec((1, QBLK, w), lambda b, q: (b, q, Z_Q1 // w)),
                  pl.BlockSpec((1, QBLK, w), lambda b, q: (b, q, Z_Q2 // w)),
                  pl.BlockSpec((1, S, w), lambda b, q: (b, 0, Z_K1 // w)),
                  pl.BlockSpec((1, S, w), lambda b, q: (b, 0, Z_K2 // w)),
                  pl.BlockSpec((1, S, 2 * w), lambda b, q: (b, 0, Z_VA // (2 * w))),
                  pl.BlockSpec((1, LANES), lambda b, q: (0, 0)),
                  pl.BlockSpec((1, LANES), lambda b, q: (0, 0)),
                  pl.BlockSpec((4, DA_DH), lambda b, q: (0, 0)),
                  pl.BlockSpec((1, LANES), lambda b, q: (0, 0))],
        out_specs=pl.BlockSpec((1, QBLK, 2 * w), lambda b, q: (b, q, 0)),
        out_shape=jax.ShapeDtypeStruct((B, S, 2 * w), BF16),
        scratch_shapes=[pltpu.VMEM((S, w), BF16), pltpu.VMEM((S, w), BF16),
                        pltpu.VMEM((S, 2 * w), BF16)],
        compiler_params=_cparams(2),
        name="diff_attention",
    )(z, z, z, z, z, tile2(qg), tile2(kg), lam_p, og.reshape(1, LANES))


def _nsa_compress_body(kc_ref, vc_ref, pet_ref, peb_ref, wt_ref, wb_ref, w2_ref, kg_ref,
                       ko_ref, vo_ref):
    for kv, (c_ref, o_ref) in enumerate(((kc_ref, ko_ref), (vc_ref, vo_ref))):
        c = c_ref[0]
        a = _dot((c + pet_ref[kv]).astype(BF16), wt_ref[kv])
        bm = _dot((c + peb_ref[kv]).astype(BF16), wb_ref[kv])
        n = c.shape[0]
        hid = jax.nn.silu(a + pltpu.roll(bm, n - 1, 0))
        out = _dot(hid.astype(BF16), w2_ref[kv])
        if kv == 0:
            out = _pair_head_norm(out, kg_ref[...])
        o_ref[0] = out.astype(BF16)


def nsa_compress(kc, vc, pet, peb, wt, wb, w2, kg):
    B, n, wdt = kc.shape
    full = lambda a: pl.BlockSpec(a.shape, lambda b: (0,) * a.ndim)
    return pl.pallas_call(
        _nsa_compress_body,
        grid=(B,),
        in_specs=[pl.BlockSpec((1, n, wdt), lambda b: (b, 0, 0)),
                  pl.BlockSpec((1, n, wdt), lambda b: (b, 0, 0)),
                  full(pet), full(peb), full(wt), full(wb), full(w2), full(kg)],
        out_specs=[pl.BlockSpec((1, n, LANES), lambda b: (b, 0, 0)),
                   pl.BlockSpec((1, n, LANES), lambda b: (b, 0, 0))],
        out_shape=[jax.ShapeDtypeStruct((B, n, LANES), BF16)] * 2,
        compiler_params=_cparams(1),
        name="nsa_compress",
    )(kc, vc, pet, peb, wt, wb, w2, kg)


def _nsa_attn_body(q_ref, kc_ref, vc_ref, ks_ref, vs_ref, kw_ref, vw_ref, gl_ref,
                   qg_ref, kg_ref, ov_ref, ex_ref, eg_ref,
                   o_ref, kss, vss, kws, vws, *, S):
    qb = pl.program_id(1)
    nb = S // SEL_LEN
    nc_pad = kc_ref.shape[1]
    n_sel = min(SEL_TOPN, nb)
    wlen = WINDOW + QBLK

    @pl.when(qb == 0)
    def _():
        kss[...] = _pair_head_norm(ks_ref[0], kg_ref[...]).astype(BF16)
        kws[...] = _pair_head_norm(kw_ref[0], kg_ref[...]).astype(BF16)
        vss[...] = vs_ref[0].astype(BF16)
        vws[...] = vw_ref[0].astype(BF16)

    upper = _half_mask()
    scale = NS_DH ** -0.5
    q0 = qb * QBLK
    pos = q0 + lax.broadcasted_iota(jnp.int32, (QBLK, 1), 0)

    cn = lax.broadcasted_iota(jnp.int32, (1, nc_pad), 1)
    n_cmp = (S - CMP_LEN) // CMP_STRIDE + 1
    c_valid = (cn * CMP_STRIDE + (CMP_LEN - 1) <= pos) & (cn < n_cmp)
    any_valid = (pos >= CMP_LEN - 1).astype(F32)

    kpos = lax.broadcasted_iota(jnp.int32, (1, S), 1)
    sdist = pos - kpos
    s_causal = sdist >= 0
    sdistf = sdist.astype(F32)
    cur = pos // SEL_LEN
    jb = lax.broadcasted_iota(jnp.int32, (1, nb), 1)
    forced = (jb == 0) | (jb == cur) | (jb == cur - 1)
    future = jb > cur

    wstart = pl.multiple_of(jnp.maximum(q0 - WINDOW, 0), QBLK)
    wpos = wstart + lax.broadcasted_iota(jnp.int32, (1, wlen), 1)
    wdist = pos - wpos
    w_ok = (wdist >= 0) & (wdist < WINDOW)
    wdistf = wdist.astype(F32)
    kw_win = kws[pl.ds(wstart, wlen), :]
    vw_win = vws[pl.ds(wstart, wlen), :]

    kc = kc_ref[0]
    vc = vc_ref[0]
    ks = kss[...]
    vs = vss[...]

    gl = gl_ref[0]
    gates = [jax.nn.sigmoid(_dot_f32_by_exact(gl, eg_ref[c])) for c in range(3)]

    qn = [_pair_head_norm(q_ref[0, :, t * LANES:(t + 1) * LANES], qg_ref[...]) * scale
          for t in range(NS_R)]

    o_cmp = [[None] * NS_KV for _ in range(NS_R)]
    o_sel = [[None] * NS_KV for _ in range(NS_R)]
    o_win = [[None] * NS_KV for _ in range(NS_R)]
    for g in range(NS_KV):
        keep = upper if g == 1 else jnp.logical_not(upper)
        qm = [jnp.where(keep, qn[t], 0.0).astype(BF16) for t in range(NS_R)]
        imp = jnp.zeros((QBLK, nb), F32)
        for t in range(NS_R):
            s_c = jnp.where(c_valid, _dot_nt(qm[t], kc), NEG_INF)
            p_c = _softmax_rows(s_c) * any_valid
            o_cmp[t][g] = _dot(p_c.astype(BF16), vc)
            imp = imp + _dot_f32_by_exact(p_c, ov_ref[...])
        score = jnp.where(future, NEG_INF, imp + FORCE_BONUS * forced.astype(F32))
        rank = jnp.zeros((QBLK, nb), jnp.int32)
        for i in range(nb):
            col_i = score[:, i:i + 1]
            ahead = (col_i > score) | ((col_i == score) & (jb > i))
            rank = rank + ahead.astype(jnp.int32)
        sel = (rank < n_sel).astype(BF16)
        key_ok = (_dot(sel, ex_ref[...]) > 0.5) & s_causal
        for t in range(NS_R):
            slope = 2.0 ** (-8.0 * (g * NS_R + t + 1) / NS_HEADS)
            s_s = jnp.where(key_ok, _dot_nt(qm[t], ks) - slope * sdistf, NEG_INF)
            o_sel[t][g] = _dot(_softmax_rows(s_s).astype(BF16), vs)
            s_w = jnp.where(w_ok, _dot_nt(qm[t], kw_win) - slope * wdistf, NEG_INF)
            o_win[t][g] = _dot(_softmax_rows(s_w).astype(BF16), vw_win)

    for t in range(NS_R):
        sl = slice(t * LANES, (t + 1) * LANES)
        pick = lambda o: jnp.where(upper, o[t][1], o[t][0])
        o = gates[0][:, sl] * pick(o_cmp) + gates[1][:, sl] * pick(o_sel) + gates[2][:, sl] * pick(o_win)
        o_ref[0, :, sl] = o.astype(BF16)


def nsa_attention(z, kcmp, vcmp, qg, kg, ov, ex, eg):
    B, S, _ = z.shape
    nq = S // QBLK
    wq = NS_HEADS * NS_DH
    body = functools.partial(_nsa_attn_body, S=S)
    zspec = lambda off: pl.BlockSpec((1, S, LANES), lambda b, q: (b, 0, off // LANES))
    full = lambda a: pl.BlockSpec(a.shape, lambda b, q: (0,) * a.ndim)
    tile2 = lambda g: jnp.tile(g, 2).reshape(1, LANES)
    qg2, kg2 = tile2(qg), tile2(kg)
    return pl.pallas_call(
        body,
        grid=(B, nq),
        in_specs=[pl.BlockSpec((1, QBLK, wq), lambda b, q: (b, q, Z_QN // wq)),
                  pl.BlockSpec((1,) + kcmp.shape[1:], lambda b, q: (b, 0, 0)),
                  pl.BlockSpec((1,) + vcmp.shape[1:], lambda b, q: (b, 0, 0)),
                  zspec(Z_KS), zspec(Z_VS), zspec(Z_KW), zspec(Z_VW),
                  pl.BlockSpec((1, QBLK, LANES), lambda b, q: (b, q, Z_GL // LANES)),
                  full(qg2), full(kg2), full(ov), full(ex), full(eg)],
        out_specs=pl.BlockSpec((1, QBLK, wq), lambda b, q: (b, q, 0)),
        out_shape=jax.ShapeDtypeStruct((B, S, wq), BF16),
        scratch_shapes=[pltpu.VMEM((S, LANES), BF16)] * 4,
        compiler_params=_cparams(2),
        name="nsa_attention",
    )(z, kcmp, vcmp, z, z, z, z, z, qg2, kg2, ov, ex, eg)


def _retention_body(q_ref, k_ref, v_ref, g_ref, og_ref, o_ref, state):
    c = pl.program_id(1)
    C = RET_CHUNK

    @pl.when(c == 0)
    def _():
        state[...] = jnp.zeros_like(state)

    jr = lax.broadcasted_iota(jnp.int32, (C, C), 0)
    jc = lax.broadcasted_iota(jnp.int32, (C, C), 1)
    rel = (jr - jc).astype(F32)
    j1 = lax.broadcasted_iota(jnp.int32, (C, 1), 0).astype(F32)
    for h in range(RET_HEADS):
        lg = math.log(1.0 - 2.0 ** (-5.0 - h))
        inner = jnp.where(rel >= 0, jnp.exp(lg * jnp.maximum(rel, 0.0)), 0.0)
        xi = jnp.exp(lg * (j1 + 1.0))
        zeta = jnp.exp(lg * (C - 1.0 - j1))
        decay_c = math.exp(lg * C)
        q = q_ref[0, :, h * RET_DK:(h + 1) * RET_DK]
        k = k_ref[0, :, h * RET_DK:(h + 1) * RET_DK] * (RET_DK ** -0.5)
        v = v_ref[0, :, h * RET_DV:(h + 1) * RET_DV].astype(BF16)
        st = state[h]
        att = _dot_nt(q.astype(BF16), k.astype(BF16)) * inner
        y = _dot(att.astype(BF16), v) + _dot((q * xi).astype(BF16), st.astype(BF16))
        state[h] = decay_c * st + _dot_tn((k * zeta).astype(BF16), v)
        ms = jnp.mean(y * y, axis=-1, keepdims=True)
        y = y * lax.rsqrt(ms + EPS) * og_ref[...]
        gate = g_ref[0, :, h * RET_DV:(h + 1) * RET_DV]
        o_ref[0, :, h * RET_DV:(h + 1) * RET_DV] = (jax.nn.silu(gate) * y).astype(BF16)


def retention(z, og):
    B, S, _ = z.shape
    nck = S // RET_CHUNK
    wk = RET_HEADS * RET_DK
    wv = RET_HEADS * RET_DV
    return pl.pallas_call(
        _retention_body,
        grid=(B, nck),
        in_specs=[pl.BlockSpec((1, RET_CHUNK, wk), lambda b, c: (b, c, 0)),
                  pl.BlockSpec((1, RET_CHUNK, wk), lambda b, c: (b, c, 1)),
                  pl.BlockSpec((1, RET_CHUNK, wv), lambda b, c: (b, c, 1)),
                  pl.BlockSpec((1, RET_CHUNK, wv), lambda b, c: (b, c, 2)),
                  pl.BlockSpec((1, RET_DV), lambda b, c: (0, 0))],
        out_specs=pl.BlockSpec((1, RET_CHUNK, wv), lambda b, c: (b, c, 0)),
        out_shape=jax.ShapeDtypeStruct((B, S, wv), BF16),
        scratch_shapes=[pltpu.VMEM((RET_HEADS, RET_DK, RET_DV), F32)],
        compiler_params=_cparams(2),
        name="retention",
    )(z, z, z, z, og.reshape(1, RET_DV))


def _router_body(h_ref, g_ref, wh_ref, wl_ref, b_ref, e_ref, p_ref):
    x = h_ref[...]
    ms = jnp.mean(x * x, axis=-1, keepdims=True)
    xn = x * lax.rsqrt(ms + EPS) * g_ref[...]
    xh, xl = _split_bf16(xn)
    logits = _dot(xh, wh_ref[...]) + _dot(xh, wl_ref[...]) + _dot(xl, wh_ref[...]) + b_ref[...]
    lane = lax.broadcasted_iota(jnp.int32, logits.shape, 1)
    vals, idxs = [], []
    for _ in range(TOP_K):
        m = jnp.max(logits, axis=-1, keepdims=True)
        idx = jnp.min(jnp.where(logits == m, lane, LANES), axis=-1, keepdims=True)
        vals.append(m)
        idxs.append(idx)
        logits = jnp.where(lane == idx, -jnp.inf, logits)
    es = [jnp.exp(v - vals[0]) for v in vals]
    tot = es[0] + es[1] + es[2] + es[3]
    e_out = jnp.zeros(lane.shape, jnp.int32)
    p_out = jnp.zeros(lane.shape, F32)
    for k in range(TOP_K):
        e_out = jnp.where(lane == k, idxs[k], e_out)
        p_out = jnp.where(lane == k, es[k] / tot, p_out)
    e_ref[...] = e_out
    p_ref[...] = p_out


def moe_router(h, g, wh, wl, b, tm=512):
    T, D = h.shape
    return pl.pallas_call(
        _router_body,
        grid=(T // tm,),
        in_specs=[pl.BlockSpec((tm, D), lambda i: (i, 0)),
                  pl.BlockSpec((1, D), lambda i: (0, 0)),
                  pl.BlockSpec((D, LANES), lambda i: (0, 0)),
                  pl.BlockSpec((D, LANES), lambda i: (0, 0)),
                  pl.BlockSpec((1, LANES), lambda i: (0, 0))],
        out_specs=[pl.BlockSpec((tm, LANES), lambda i: (i, 0)),
                   pl.BlockSpec((tm, LANES), lambda i: (i, 0))],
        out_shape=[jax.ShapeDtypeStruct((T, LANES), jnp.int32),
                   jax.ShapeDtypeStruct((T, LANES), F32)],
        compiler_params=_cparams(1),
        name="moe_router",
    )(h, g.reshape(1, D), wh, wl, b)


def _gather_rows(idx_ref, src_hbm, dst, sem, n):
    def copy(r):
        return pltpu.make_async_copy(src_hbm.at[pl.ds(idx_ref[0, 0, r], 1), :],
                                     dst.at[pl.ds(r, 1), :], sem)

    def start(r, carry):
        copy(r).start()
        return carry

    def wait(r, carry):
        copy(r).wait()
        return carry

    lax.fori_loop(0, n, start, 0)
    lax.fori_loop(0, n, wait, 0)


def _moe_expert_body(blk_e_ref, tok_ref, h_hbm, g_ref, wgu_ref, bgu_ref, wd_ref, bd_ref,
                     o_ref, xbuf, sem):
    del blk_e_ref
    tm = xbuf.shape[0]
    _gather_rows(tok_ref, h_hbm, xbuf, sem, tm)
    x = xbuf[...]
    ms = jnp.mean(x * x, axis=-1, keepdims=True)
    xn = (x * lax.rsqrt(ms + EPS) * g_ref[...]).astype(BF16)
    hgl = _dot(xn, wgu_ref[0]) + bgu_ref[0]
    hg = jnp.minimum(hgl[:, :D_FF], SWIGLU_LIMIT)
    hl = jnp.clip(hgl[:, D_FF:], -SWIGLU_LIMIT, SWIGLU_LIMIT)
    a = hg * jax.nn.sigmoid(SWIGLU_ALPHA * hg) * (hl + 1.0)
    o_ref[...] = _dot(a.astype(BF16), wd_ref[0]) + bd_ref[0]


def moe_experts(blk_e, slot_tok, h, g, wgu, bgu, wd, bd):
    T, D = h.shape
    n_blocks = blk_e.shape[0]
    tm = MOE_TM
    grid_spec = pltpu.PrefetchScalarGridSpec(
        num_scalar_prefetch=1,
        grid=(n_blocks,),
        in_specs=[pl.BlockSpec((1, 1, tm), lambda i, be: (i, 0, 0), memory_space=pltpu.SMEM),
                  pl.BlockSpec(memory_space=pl.ANY),
                  pl.BlockSpec((1, D), lambda i, be: (0, 0)),
                  pl.BlockSpec((1, D, 2 * D_FF), lambda i, be: (be[i], 0, 0)),
                  pl.BlockSpec((1, 1, 2 * D_FF), lambda i, be: (be[i], 0, 0)),
                  pl.BlockSpec((1, D_FF, D), lambda i, be: (be[i], 0, 0)),
                  pl.BlockSpec((1, 1, D), lambda i, be: (be[i], 0, 0))],
        out_specs=pl.BlockSpec((tm, D), lambda i, be: (i, 0)),
        scratch_shapes=[pltpu.VMEM((tm, D), F32), pltpu.SemaphoreType.DMA(())],
    )
    return pl.pallas_call(
        _moe_expert_body,
        grid_spec=grid_spec,
        out_shape=jax.ShapeDtypeStruct((n_blocks * tm, D), F32),
        compiler_params=_cparams(1),
        name="moe_experts",
    )(blk_e, slot_tok.reshape(n_blocks, 1, tm), h, g.reshape(1, D), wgu,
      bgu.reshape(N_EXPERTS, 1, 2 * D_FF), wd, bd.reshape(N_EXPERTS, 1, D))


def _combine_ple_body(slot_ref, yb_hbm, h_ref, gate_ref, p_ref, wg_ref, wp_ref,
                      o_ref, ybuf, sem):
    tm = h_ref.shape[0]
    _gather_rows(slot_ref, yb_hbm, ybuf, sem, TOP_K * tm)
    gate = gate_ref[...]
    h = h_ref[...]
    for k in range(TOP_K):
        h = h + gate[:, k:k + 1] * ybuf[k * tm:(k + 1) * tm, :]
    ms = jnp.mean(h * h, axis=-1, keepdims=True)
    hn = (h * lax.rsqrt(ms + EPS)).astype(BF16)
    pg = jax.nn.sigmoid(_dot(hn, wg_ref[...]))
    o_ref[...] = h + pg * _dot(p_ref[...].astype(BF16), wp_ref[...])


def combine_ple(slots, yb, h, gate, p, wg, wp):
    T, D = h.shape
    tm = CMB_TM
    nblk = T // tm
    return pl.pallas_call(
        _combine_ple_body,
        grid=(nblk,),
        in_specs=[pl.BlockSpec((1, 1, TOP_K * tm), lambda i: (i, 0, 0), memory_space=pltpu.SMEM),
                  pl.BlockSpec(memory_space=pl.ANY),
                  pl.BlockSpec((tm, D), lambda i: (i, 0)),
                  pl.BlockSpec((tm, LANES), lambda i: (i, 0)),
                  pl.BlockSpec((tm, PLE_DIM), lambda i: (i, 0)),
                  pl.BlockSpec((D, D), lambda i: (0, 0)),
                  pl.BlockSpec((PLE_DIM, D), lambda i: (0, 0))],
        out_specs=pl.BlockSpec((tm, D), lambda i: (i, 0)),
        out_shape=jax.ShapeDtypeStruct((T, D), F32),
        scratch_shapes=[pltpu.VMEM((TOP_K * tm, D), F32), pltpu.SemaphoreType.DMA(())],
        compiler_params=_cparams(1),
        name="combine_ple",
    )(slots, yb, h, gate, p, wg, wp)


def _routing_plan(top_e, T):
    tm = MOE_TM
    n = T * TOP_K
    flat_e = top_e.reshape(-1)
    order = jnp.argsort(flat_e)
    e_sorted = flat_e[order]
    tok = (order // TOP_K).astype(jnp.int32)
    counts = jnp.bincount(flat_e, length=N_EXPERTS)
    padded = (counts + tm - 1) // tm * tm
    pad_end = jnp.cumsum(padded)
    pad_start = pad_end - padded
    start = jnp.cumsum(counts) - counts
    dest = (pad_start[e_sorted] + jnp.arange(n) - start[e_sorted]).astype(jnp.int32)
    n_blocks = n // tm + N_EXPERTS
    slot_tok = jnp.zeros((n_blocks * tm,), jnp.int32).at[dest].set(tok)
    blk_e = jnp.minimum(jnp.searchsorted(pad_end, jnp.arange(n_blocks) * tm, side='right'),
                        N_EXPERTS - 1).astype(jnp.int32)
    slot_of = jnp.zeros((n,), jnp.int32).at[order].set(dest).reshape(T, TOP_K)
    return blk_e, slot_tok, slot_of


def moe_ple_layer(h, p_i, g2, router_w, router_b, wgu, bgu, wd, bd, w_proj, w_gate):
    T, D = h.shape
    rw = jnp.pad(router_w, ((0, 0), (0, LANES - N_EXPERTS)))
    rwh = rw.astype(BF16)
    rwl = (rw - rwh.astype(F32)).astype(BF16)
    rb = jnp.pad(router_b, (0, LANES - N_EXPERTS), constant_values=NEG_INF).reshape(1, LANES)
    top_e, gate = moe_router(h, g2, rwh, rwl, rb)
    blk_e, slot_tok, slot_of = _routing_plan(top_e[:, :TOP_K], T)
    yb = moe_experts(blk_e, slot_tok, h, g2, wgu.astype(BF16), bgu, wd.astype(BF16), bd)
    nblk = T // CMB_TM
    slots = slot_of.reshape(nblk, CMB_TM, TOP_K).transpose(0, 2, 1).reshape(nblk, 1, TOP_K * CMB_TM)
    return combine_ple(slots, yb, h, gate, p_i, w_gate.astype(BF16), w_proj.astype(BF16))


def _even_layout():
    idx = np.full((EV_PAD,), EV_IN, np.int64)
    idx[:Z_QN] = np.arange(Z_QN)
    qn_cols = np.zeros((NS_HEADS * NS_DH,), np.int64)
    for c in range(NS_HEADS * NS_DH):
        t, half, d = c // LANES, (c % LANES) // NS_DH, c % NS_DH
        qn_cols[c] = (half * NS_R + t) * NS_DH + d
    idx[Z_QN:Z_KC] = Z_QN + qn_cols
    idx[Z_KC:Z_GL] = np.arange(Z_KC, Z_GL)
    idx[Z_GL:Z_GL + NS_HEADS * 3] = np.arange(Z_GL, EV_IN)
    eg = np.zeros((3, LANES, NS_HEADS * NS_DH), np.float32)
    for c in range(NS_HEADS * NS_DH):
        head = qn_cols[c] // NS_DH
        for br in range(3):
            eg[br, 3 * head + br, c] = 1.0
    return idx, qn_cols, eg


def _nsa_constants(S):
    nc = (S - CMP_LEN) // CMP_STRIDE + 1
    nc_pad = S // CMP_STRIDE
    nb = S // SEL_LEN
    cstart = np.arange(nc_pad) * CMP_STRIDE
    bstart = np.arange(nb) * SEL_LEN
    lo = np.maximum(cstart[:, None], bstart[None, :])
    hi = np.minimum(cstart[:, None] + CMP_LEN, bstart[None, :] + SEL_LEN)
    ov = np.maximum(hi - lo, 0).astype(np.float32) / CMP_LEN
    ov[nc:] = 0.0
    ex = (np.arange(S)[None, :] // SEL_LEN == np.arange(nb)[:, None]).astype(np.float32)
    return ov, ex


def _compress_weights(pe, w1, w2):
    half = CMP_LEN // 2
    w1r = w1.reshape(2, CMP_LEN, NS_DH, CMP_HID)
    zeros = jnp.zeros((2, half, NS_DH, CMP_HID), w1.dtype)

    def expand(part):
        g0 = jnp.stack([part, zeros], axis=2)
        g1 = jnp.stack([zeros, part], axis=2)
        return jnp.concatenate([g0, g1], axis=-1).reshape(2, half * LANES, NS_KV * CMP_HID)

    wt = expand(w1r[:, :half]).astype(BF16)
    wb = expand(w1r[:, half:]).astype(BF16)
    z2 = jnp.zeros_like(w2)
    w2b = jnp.concatenate([jnp.concatenate([w2, z2], axis=-1),
                           jnp.concatenate([z2, w2], axis=-1)], axis=1).astype(BF16)
    tilepe = lambda part: jnp.tile(part[:, :, None, :], (1, 1, NS_KV, 1)).reshape(2, 1, half * LANES)
    return tilepe(pe[:, :half]), tilepe(pe[:, half:]), wt, wb, w2b


def even_mixer_layer(h, B, S, g1, w_in, w_out, da_qn_g, da_kn_g, da_lam, da_out_g,
                     ns_qn_g, ns_kn_g, ns_cmp_pe, ns_cmp_w1, ns_cmp_w2, layer):
    T, D = h.shape
    idx, qn_cols, eg = _even_layout()
    w_in_p = jnp.take(jnp.pad(w_in, ((0, 0), (0, 1))), jnp.asarray(idx), axis=1).astype(BF16)
    z = norm_matmul(h, g1, w_in_p).reshape(B, S, EV_PAD)
    lam_init = 0.8 - 0.6 * math.exp(-0.3 * layer)
    oa = diff_attention(z, da_qn_g, da_kn_g, da_lam, da_out_g, lam_init)
    n_chunk = S // CMP_STRIDE
    kc = z[:, :, Z_KC:Z_KC + LANES].reshape(B, n_chunk, CMP_STRIDE * LANES)
    vc = z[:, :, Z_VC:Z_VC + LANES].reshape(B, n_chunk, CMP_STRIDE * LANES)
    pet, peb, wt, wb, w2b = _compress_weights(ns_cmp_pe, ns_cmp_w1, ns_cmp_w2)
    kg2 = jnp.tile(ns_kn_g, 2).reshape(1, LANES)
    kcmp, vcmp = nsa_compress(kc, vc, pet, peb, wt, wb, w2b, kg2)
    ov, ex = _nsa_constants(S)
    ob = nsa_attention(z, kcmp, vcmp, ns_qn_g, ns_kn_g, jnp.asarray(ov, BF16),
                       jnp.asarray(ex, BF16), jnp.asarray(eg, BF16))
    wa = w_out[:DA_HEADS * 2 * DA_DH].astype(BF16)
    wb_out = jnp.take(w_out[DA_HEADS * 2 * DA_DH:], jnp.asarray(qn_cols), axis=0).astype(BF16)
    return out_proj2(h, oa.reshape(T, -1), ob.reshape(T, -1), wa, wb_out)


def odd_mixer_layer(h, B, S, g1, w_in, w_out, out_g):
    T, D = h.shape
    z = norm_matmul(h, g1, w_in.astype(BF16)).reshape(B, S, OD_IN)
    y = retention(z, out_g)
    return out_proj1(h, y.reshape(T, -1), w_out.astype(BF16))


def kernel(x, p, norm1_g, norm2_g, ev_w_in, ev_w_out, da_qn_g, da_kn_g, da_lam, da_out_g,
           ns_qn_g, ns_kn_g, ns_cmp_pe, ns_cmp_w1, ns_cmp_w2, od_w_in, od_w_out, ret_out_g,
           router_w, router_b, moe_w_gu, moe_b_gu, moe_w_down, moe_b_down, ple_w_proj, ple_w_gate):
    B, S, D = x.shape
    T = B * S
    h = x.reshape(T, D)
    for i in range(DEPTH):
        if i % 2 == 0:
            e = i // 2
            h = even_mixer_layer(h, B, S, norm1_g[i], ev_w_in[e], ev_w_out[e], da_qn_g[e], da_kn_g[e],
                                 da_lam[e], da_out_g[e], ns_qn_g[e], ns_kn_g[e], ns_cmp_pe[e],
                                 ns_cmp_w1[e], ns_cmp_w2[e], i)
        else:
            o = i // 2
            h = odd_mixer_layer(h, B, S, norm1_g[i], od_w_in[o], od_w_out[o], ret_out_g[o])
        h = moe_ple_layer(h, p[i].reshape(T, PLE_DIM), norm2_g[i], router_w[i], router_b[i],
                          moe_w_gu[i], moe_b_gu[i], moe_w_down[i], moe_b_down[i],
                          ple_w_proj[i], ple_w_gate[i])
    return h.reshape(B, S, D)
```

```python
import functools
import math

import numpy as np
import jax
import jax.numpy as jnp
from jax import lax
from jax.experimental import pallas as pl
from jax.experimental.pallas import tpu as pltpu

F32 = jnp.float32
BF16 = jnp.bfloat16

D_MODEL = 1024
DEPTH = 4
PLE_DIM = 256
QBLK = 128
NEG_INF = -1e30
EPS = 1e-6
DA_DH = 64
DA_HEADS = 4
NS_DH = 64
NS_HEADS = 8
NS_KV = 2
NS_R = NS_HEADS // NS_KV
CMP_LEN = 32
CMP_STRIDE = 16
CMP_HID = 256
SEL_LEN = 64
SEL_TOPN = 16
WINDOW = 512
FORCE_BONUS = 1e4
RET_DK = 256
RET_HEADS = 4
RET_DV = 512
RET_CHUNK = 128
N_EXPERTS = 32
TOP_K = 4
D_FF = 1024
SWIGLU_ALPHA = 1.702
SWIGLU_LIMIT = 7.0

LANES = 128
VMEM_LIMIT = 56 * 1024 * 1024

Z_Q1, Z_Q2, Z_K1, Z_K2, Z_VA, Z_QN = 0, 256, 512, 768, 1024, 1536
Z_KC, Z_VC, Z_KS, Z_VS, Z_KW, Z_VW, Z_GL = 2048, 2176, 2304, 2432, 2560, 2688, 2816
EV_IN = 2840
EV_PAD = 3072
OD_IN = 6144

KV_CHUNK = 256
ROUTE_TM = 512
MOE_TM = 512
CMB_TM = 128
DMA_UNROLL = 8


def _cparams(n_axes):
    return pltpu.CompilerParams(dimension_semantics=("arbitrary",) * n_axes,
                                vmem_limit_bytes=VMEM_LIMIT)


def _dot(a, b):
    return jnp.dot(a, b, preferred_element_type=F32)


def _dot_nt(a, b):
    return lax.dot_general(a, b, (((1,), (1,)), ((), ())), preferred_element_type=F32)


def _dot_tn(a, b):
    return lax.dot_general(a, b, (((0,), (0,)), ((), ())), preferred_element_type=F32)


def _split_bf16(x):
    hi = x.astype(BF16)
    lo = (x - hi.astype(F32)).astype(BF16)
    return hi, lo


def _half_mask():
    return lax.broadcasted_iota(jnp.int32, (1, LANES), 1) >= 64


def _pair_head_norm(x, gain):
    upper = _half_mask()
    x2 = x * x
    s_lo = jnp.sum(jnp.where(upper, 0.0, x2), axis=-1, keepdims=True)
    s_hi = jnp.sum(jnp.where(upper, x2, 0.0), axis=-1, keepdims=True)
    r = jnp.where(upper, lax.rsqrt(s_hi / 64.0 + EPS), lax.rsqrt(s_lo / 64.0 + EPS))
    return x * r * gain


def _online_softmax_step(s, m_ref, l_ref):
    m_old = m_ref[...]
    m_new = jnp.maximum(m_old, jnp.max(s, axis=-1, keepdims=True))
    alpha = jnp.exp(m_old - m_new)
    e = jnp.exp(s - m_new)
    l_ref[...] = alpha * l_ref[...] + jnp.sum(e, axis=-1, keepdims=True)
    m_ref[...] = m_new
    return alpha, e


def _norm_matmul_body(x_ref, g_ref, w_ref, o_ref, *, tn):
    x = x_ref[...]
    ms = jnp.mean(x * x, axis=-1, keepdims=True)
    xn = (x * lax.rsqrt(ms + EPS) * g_ref[...]).astype(BF16)
    for j in range(w_ref.shape[1] // tn):
        o_ref[:, j * tn:(j + 1) * tn] = _dot(xn, w_ref[:, j * tn:(j + 1) * tn]).astype(o_ref.dtype)


def norm_matmul(x, g, w, tm=512, tn=512):
    T, D = x.shape
    N = w.shape[1]
    return pl.pallas_call(
        functools.partial(_norm_matmul_body, tn=tn),
        grid=(T // tm,),
        in_specs=[pl.BlockSpec((tm, D), lambda i: (i, 0)),
                  pl.BlockSpec((1, D), lambda i: (0, 0)),
                  pl.BlockSpec((D, N), lambda i: (0, 0))],
        out_specs=pl.BlockSpec((tm, N), lambda i: (i, 0)),
        out_shape=jax.ShapeDtypeStruct((T, N), BF16),
        compiler_params=_cparams(1),
        name="norm_matmul",
    )(x, g.reshape(1, D), w)


def _out_proj2_body(h_ref, a_ref, b_ref, wa_ref, wb_ref, o_ref):
    o_ref[...] = h_ref[...] + _dot(a_ref[...], wa_ref[...]) + _dot(b_ref[...], wb_ref[...])


def out_proj2(h, a, b, wa, wb, tm=512):
    T, D = h.shape
    return pl.pallas_call(
        _out_proj2_body,
        grid=(T // tm,),
        in_specs=[pl.BlockSpec((tm, D), lambda i: (i, 0)),
                  pl.BlockSpec((tm, a.shape[1]), lambda i: (i, 0)),
                  pl.BlockSpec((tm, b.shape[1]), lambda i: (i, 0)),
                  pl.BlockSpec(wa.shape, lambda i: (0, 0)),
                  pl.BlockSpec(wb.shape, lambda i: (0, 0))],
        out_specs=pl.BlockSpec((tm, D), lambda i: (i, 0)),
        out_shape=jax.ShapeDtypeStruct((T, D), F32),
        compiler_params=_cparams(1),
        name="out_proj2",
    )(h, a, b, wa, wb)


def _out_proj1_body(h_ref, a_ref, wa_ref, o_ref):
    o_ref[...] = h_ref[...] + _dot(a_ref[...], wa_ref[...])


def out_proj1(h, a, wa, tm=512):
    T, D = h.shape
    return pl.pallas_call(
        _out_proj1_body,
        grid=(T // tm,),
        in_specs=[pl.BlockSpec((tm, D), lambda i: (i, 0)),
                  pl.BlockSpec((tm, a.shape[1]), lambda i: (i, 0)),
                  pl.BlockSpec(wa.shape, lambda i: (0, 0))],
        out_specs=pl.BlockSpec((tm, D), lambda i: (i, 0)),
        out_shape=jax.ShapeDtypeStruct((T, D), F32),
        compiler_params=_cparams(1),
        name="out_proj1",
    )(h, a, wa)


def _diff_attn_body(q1_ref, q2_ref, k1_ref, k2_ref, v_ref, qg_ref, kg_ref, lam_ref, og_ref,
                    o_ref, k1s, k2s, vs, qst, m_s, l_s, acc_s, *, lam_init):
    qb = pl.program_id(1)
    n_pairs = DA_HEADS // 2

    @pl.when(qb == 0)
    def _():
        for p in range(n_pairs):
            sl = slice(p * LANES, (p + 1) * LANES)
            k1s[:, sl] = _pair_head_norm(k1_ref[0, :, sl].astype(F32), kg_ref[...]).astype(BF16)
            k2s[:, sl] = _pair_head_norm(k2_ref[0, :, sl].astype(F32), kg_ref[...]).astype(BF16)
        vs[...] = v_ref[0]

    upper = _half_mask()
    scale = DA_DH ** -0.5
    for p in range(n_pairs):
        sl = slice(p * LANES, (p + 1) * LANES)
        for which, q_ref in enumerate((q1_ref, q2_ref)):
            q = _pair_head_norm(q_ref[0, :, sl].astype(F32), qg_ref[...]) * scale
            qst[p * 2 + which] = jnp.concatenate(
                [jnp.where(upper, 0.0, q), jnp.where(upper, q, 0.0)], axis=0).astype(BF16)
    m_s[...] = jnp.full(m_s.shape, NEG_INF, F32)
    l_s[...] = jnp.zeros(l_s.shape, F32)
    acc_s[...] = jnp.zeros(acc_s.shape, F32)

    q0 = qb * QBLK
    pos = q0 + lax.broadcasted_iota(jnp.int32, (QBLK, 1), 0)

    def chunk(c, carry):
        k0 = pl.multiple_of(c * KV_CHUNK, KV_CHUNK)
        dist = pos - (k0 + lax.broadcasted_iota(jnp.int32, (QBLK, KV_CHUNK), 1))
        causal = dist >= 0
        distf = dist.astype(F32)
        for p in range(n_pairs):
            sl = slice(p * LANES, (p + 1) * LANES)
            for which, ks in enumerate((k1s, k2s)):
                idx = p * 2 + which
                s = _dot_nt(qst[idx], ks[pl.ds(k0, KV_CHUNK), sl])
                s = jnp.concatenate(
                    [jnp.where(causal, s[j * QBLK:(j + 1) * QBLK]
                               - 2.0 ** (-8.0 * (2 * p + j + 1) / DA_HEADS) * distf, NEG_INF)
                     for j in range(2)], axis=0)
                alpha, e = _online_softmax_step(s, m_s.at[idx], l_s.at[idx])
                eb = e.astype(BF16)
                for j in range(2):
                    h = 2 * p + j
                    rows = slice(j * QBLK, (j + 1) * QBLK)
                    pv = _dot(eb[rows], vs[pl.ds(k0, KV_CHUNK), h * 2 * DA_DH:(h + 1) * 2 * DA_DH])
                    acc_s[idx, rows, :] = alpha[rows] * acc_s[idx, rows, :] + pv
        return carry

    lax.fori_loop(0, (q0 + QBLK + KV_CHUNK - 1) // KV_CHUNK, chunk, 0)

    lam_p = lam_ref[...]
    lam = (jnp.exp(jnp.sum(lam_p[0:1] * lam_p[1:2], axis=-1, keepdims=True))
           - jnp.exp(jnp.sum(lam_p[2:3] * lam_p[3:4], axis=-1, keepdims=True)) + lam_init)
    for p in range(n_pairs):
        for j in range(2):
            h = 2 * p + j
            rows = slice(j * QBLK, (j + 1) * QBLK)
            o = (acc_s[2 * p, rows, :] / l_s[2 * p, rows, :]
                 - lam * (acc_s[2 * p + 1, rows, :] / l_s[2 * p + 1, rows, :]))
            ms = jnp.mean(o * o, axis=-1, keepdims=True)
            o = o * lax.rsqrt(ms + EPS) * og_ref[...] * (1.0 - lam_init)
            o_ref[0, :, h * 2 * DA_DH:(h + 1) * 2 * DA_DH] = o.astype(BF16)


def diff_attention(z, qg, kg, lam_p, og, lam_init):
    B, S, _ = z.shape
    nq = S // QBLK
    w = DA_HEADS * DA_DH
    body = functools.partial(_diff_attn_body, lam_init=lam_init)
    tile2 = lambda g: jnp.tile(g, 2).reshape(1, LANES)
    return pl.pallas_call(
        body,
        grid=(B, nq),
        in_specs=[pl.BlockSpec((1, QBLK, w), lambda b, q: (b, q, Z_Q1 // w)),
                  pl.BlockSpec((1, QBLK, w), lambda b, q: (b, q, Z_Q2 // w)),
                  pl.BlockSpec((1, S, w), lambda b, q: (b, 0, Z_K1 // w)),
                  pl.BlockSpec((1, S, w), lambda b, q: (b, 0, Z_K2 // w)),
                  pl.BlockSpec((1, S, 2 * w), lambda b, q: (b, 0, Z_VA // (2 * w))),
                  pl.BlockSpec((1, LANES), lambda b, q: (0, 0)),
                  pl.BlockSpec((1, LANES), lambda b, q: (0, 0)),
                  pl.BlockSpec((4, DA_DH), lambda b, q: (0, 0)),
                  pl.BlockSpec((1, LANES), lambda b, q: (0, 0))],
        out_specs=pl.BlockSpec((1, QBLK, 2 * w), lambda b, q: (b, q, 0)),
        out_shape=jax.ShapeDtypeStruct((B, S, 2 * w), BF16),
        scratch_shapes=[pltpu.VMEM((S, w), BF16), pltpu.VMEM((S, w), BF16),
                        pltpu.VMEM((S, 2 * w), BF16),
                        pltpu.VMEM((DA_HEADS, 2 * QBLK, LANES), BF16),
                        pltpu.VMEM((DA_HEADS, 2 * QBLK, 1), F32),
                        pltpu.VMEM((DA_HEADS, 2 * QBLK, 1), F32),
                        pltpu.VMEM((DA_HEADS, 2 * QBLK, LANES), F32)],
        compiler_params=_cparams(2),
        name="diff_attention",
    )(z, z, z, z, z, tile2(qg), tile2(kg), lam_p, og.reshape(1, LANES))


def _nsa_compress_body(kc_ref, vc_ref, pet_ref, peb_ref, wt_ref, wb_ref, w2_ref, kg_ref,
                       ko_ref, vo_ref):
    for kv, (c_ref, o_ref) in enumerate(((kc_ref, ko_ref), (vc_ref, vo_ref))):
        c = c_ref[0].astype(F32)
        a = _dot((c + pet_ref[kv]).astype(BF16), wt_ref[kv])
        bm = _dot((c + peb_ref[kv]).astype(BF16), wb_ref[kv])
        n = c.shape[0]
        hid = jax.nn.silu(a + pltpu.roll(bm, n - 1, 0))
        out = _dot(hid.astype(BF16), w2_ref[kv])
        if kv == 0:
            out = _pair_head_norm(out, kg_ref[...])
        o_ref[0] = out.astype(BF16)


def nsa_compress(kc, vc, pet, peb, wt, wb, w2, kg):
    B, n, wdt = kc.shape
    full = lambda a: pl.BlockSpec(a.shape, lambda b: (0,) * a.ndim)
    return pl.pallas_call(
        _nsa_compress_body,
        grid=(B,),
        in_specs=[pl.BlockSpec((1, n, wdt), lambda b: (b, 0, 0)),
                  pl.BlockSpec((1, n, wdt), lambda b: (b, 0, 0)),
                  full(pet), full(peb), full(wt), full(wb), full(w2), full(kg)],
        out_specs=[pl.BlockSpec((1, n, LANES), lambda b: (b, 0, 0)),
                   pl.BlockSpec((1, n, LANES), lambda b: (b, 0, 0))],
        out_shape=[jax.ShapeDtypeStruct((B, n, LANES), BF16)] * 2,
        compiler_params=_cparams(1),
        name="nsa_compress",
    )(kc, vc, pet, peb, wt, wb, w2, kg)


def _nsa_attn_body(q_ref, kc_ref, vc_ref, ks_ref, vs_ref, kw_ref, vw_ref, gl_ref,
                   qg_ref, kg_ref, ovt_ref, ex_ref, eg_ref,
                   o_ref, kss, kws, qst, sel_s, m_s, l_s, acc_s, *, S):
    qb = pl.program_id(1)
    nb = S // SEL_LEN
    nc_pad = kc_ref.shape[1]
    n_sel = min(SEL_TOPN, nb)
    wlen = WINDOW + QBLK
    M = NS_R * QBLK

    @pl.when(qb == 0)
    def _():
        kss[...] = _pair_head_norm(ks_ref[0].astype(F32), kg_ref[...]).astype(BF16)
        kws[...] = _pair_head_norm(kw_ref[0].astype(F32), kg_ref[...]).astype(BF16)

    upper = _half_mask()
    scale = NS_DH ** -0.5
    q0 = qb * QBLK
    pos = q0 + lax.broadcasted_iota(jnp.int32, (QBLK, 1), 0)
    pos_row = q0 + lax.broadcasted_iota(jnp.int32, (1, QBLK), 1)

    qn = [_pair_head_norm(q_ref[0, :, t * LANES:(t + 1) * LANES].astype(F32), qg_ref[...]) * scale
          for t in range(NS_R)]
    for g in range(NS_KV):
        keep = upper if g == 1 else jnp.logical_not(upper)
        qst[g] = jnp.concatenate([jnp.where(keep, qn[t], 0.0) for t in range(NS_R)], axis=0).astype(BF16)
    m_s[...] = jnp.full(m_s.shape, NEG_INF, F32)
    l_s[...] = jnp.zeros(l_s.shape, F32)
    acc_s[...] = jnp.zeros(acc_s.shape, F32)

    cn = lax.broadcasted_iota(jnp.int32, (1, nc_pad), 1)
    n_cmp = (S - CMP_LEN) // CMP_STRIDE + 1
    c_valid = (cn * CMP_STRIDE + (CMP_LEN - 1) <= pos) & (cn < n_cmp)
    any_valid = (pos >= CMP_LEN - 1).astype(F32)
    jbt = lax.broadcasted_iota(jnp.int32, (nb, QBLK), 0)
    cur = pos_row // SEL_LEN
    forced = ((jbt == 0) | (jbt == cur) | (jbt == cur - 1)).astype(F32)
    future = jbt > cur
    kc = kc_ref[0]
    vc = vc_ref[0]
    o_cmp = []
    for g in range(NS_KV):
        s_c = _dot_nt(qst[g], kc)
        ps = []
        for t in range(NS_R):
            s_t = jnp.where(c_valid, s_c[t * QBLK:(t + 1) * QBLK], NEG_INF)
            e = jnp.exp(s_t - jnp.max(s_t, axis=-1, keepdims=True))
            ps.append(e * (any_valid / jnp.sum(e, axis=-1, keepdims=True)))
        o_cmp.append(_dot(jnp.concatenate(ps, axis=0).astype(BF16), vc))
        p_hi, p_lo = _split_bf16(ps[0] + ps[1] + ps[2] + ps[3])
        imp = _dot_nt(ovt_ref[...], p_hi) + _dot_nt(ovt_ref[...], p_lo)
        score = jnp.where(future, NEG_INF, imp + FORCE_BONUS * forced)
        rank = jnp.zeros((nb, QBLK), jnp.int32)
        for i in range(nb):
            row_i = score[i:i + 1, :]
            ahead = (row_i > score) | ((row_i == score) & (jbt > i))
            rank = rank + ahead.astype(jnp.int32)
        sel_t = jnp.concatenate([(rank < n_sel).astype(F32),
                                 jnp.zeros((LANES - nb, QBLK), F32)], axis=0)
        sel_s[g] = sel_t.T.astype(BF16)

    def chunk(c, carry):
        k0 = pl.multiple_of(c * KV_CHUNK, KV_CHUNK)
        kch = kss[pl.ds(k0, KV_CHUNK), :]
        vch = vs_ref[0, pl.ds(k0, KV_CHUNK), :]
        dist = pos - (k0 + lax.broadcasted_iota(jnp.int32, (QBLK, KV_CHUNK), 1))
        causal = dist >= 0
        distf = dist.astype(F32)
        for g in range(NS_KV):
            ok = (_dot(sel_s[g], ex_ref[c]) > 0.5) & causal
            s = _dot_nt(qst[g], kch)
            s = jnp.concatenate(
                [jnp.where(ok, s[t * QBLK:(t + 1) * QBLK]
                           - 2.0 ** (-8.0 * (g * NS_R + t + 1) / NS_HEADS) * distf, NEG_INF)
                 for t in range(NS_R)], axis=0)
            alpha, e = _online_softmax_step(s, m_s.at[g], l_s.at[g])
            acc_s[g] = alpha * acc_s[g] + _dot(e.astype(BF16), vch)
        return carry

    lax.fori_loop(0, (q0 + QBLK + KV_CHUNK - 1) // KV_CHUNK, chunk, 0)

    wstart = pl.multiple_of(jnp.maximum(q0 - WINDOW, 0), QBLK)
    wdist = pos - (wstart + lax.broadcasted_iota(jnp.int32, (QBLK, wlen), 1))
    w_ok = (wdist >= 0) & (wdist < WINDOW)
    wdistf = wdist.astype(F32)
    kw_win = kws[pl.ds(wstart, wlen), :]
    vw_win = vw_ref[0, pl.ds(wstart, wlen), :]
    o_win = []
    for g in range(NS_KV):
        s = _dot_nt(qst[g], kw_win)
        s = jnp.concatenate(
            [jnp.where(w_ok, s[t * QBLK:(t + 1) * QBLK]
                       - 2.0 ** (-8.0 * (g * NS_R + t + 1) / NS_HEADS) * wdistf, NEG_INF)
             for t in range(NS_R)], axis=0)
        e = jnp.exp(s - jnp.max(s, axis=-1, keepdims=True))
        o_win.append(_dot(e.astype(BF16), vw_win) / jnp.sum(e, axis=-1, keepdims=True))

    gl_hi, gl_lo = _split_bf16(gl_ref[0].astype(F32))
    gates = [jax.nn.sigmoid(_dot(gl_hi, eg_ref[c]) + _dot(gl_lo, eg_ref[c])) for c in range(3)]
    o_sel = [acc_s[g] / l_s[g] for g in range(NS_KV)]
    for t in range(NS_R):
        sl = slice(t * LANES, (t + 1) * LANES)
        rows = slice(t * QBLK, (t + 1) * QBLK)
        pick = lambda o: jnp.where(upper, o[1][rows], o[0][rows])
        o = gates[0][:, sl] * pick(o_cmp) + gates[1][:, sl] * pick(o_sel) + gates[2][:, sl] * pick(o_win)
        o_ref[0, :, sl] = o.astype(BF16)


def nsa_attention(z, kcmp, vcmp, qg, kg, ovt, ex, eg):
    B, S, _ = z.shape
    nq = S // QBLK
    wq = NS_HEADS * NS_DH
    M = NS_R * QBLK
    body = functools.partial(_nsa_attn_body, S=S)
    zspec = lambda off: pl.BlockSpec((1, S, LANES), lambda b, q: (b, 0, off // LANES))
    full = lambda a: pl.BlockSpec(a.shape, lambda b, q: (0,) * a.ndim)
    tile2 = lambda g: jnp.tile(g, 2).reshape(1, LANES)
    qg2, kg2 = tile2(qg), tile2(kg)
    return pl.pallas_call(
        body,
        grid=(B, nq),
        in_specs=[pl.BlockSpec((1, QBLK, wq), lambda b, q: (b, q, Z_QN // wq)),
                  pl.BlockSpec((1,) + kcmp.shape[1:], lambda b, q: (b, 0, 0)),
                  pl.BlockSpec((1,) + vcmp.shape[1:], lambda b, q: (b, 0, 0)),
                  zspec(Z_KS), zspec(Z_VS), zspec(Z_KW), zspec(Z_VW),
                  pl.BlockSpec((1, QBLK, LANES), lambda b, q: (b, q, Z_GL // LANES)),
                  full(qg2), full(kg2), full(ovt), full(ex), full(eg)],
        out_specs=pl.BlockSpec((1, QBLK, wq), lambda b, q: (b, q, 0)),
        out_shape=jax.ShapeDtypeStruct((B, S, wq), BF16),
        scratch_shapes=[pltpu.VMEM((S, LANES), BF16), pltpu.VMEM((S, LANES), BF16),
                        pltpu.VMEM((NS_KV, M, LANES), BF16),
                        pltpu.VMEM((NS_KV, QBLK, LANES), BF16),
                        pltpu.VMEM((NS_KV, M, 1), F32), pltpu.VMEM((NS_KV, M, 1), F32),
                        pltpu.VMEM((NS_KV, M, LANES), F32)],
        compiler_params=_cparams(2),
        name="nsa_attention",
    )(z, kcmp, vcmp, z, z, z, z, z, qg2, kg2, ovt, ex, eg)


def _retention_body(q_ref, k_ref, v_ref, g_ref, og_ref, o_ref, state):
    c = pl.program_id(1)
    C = RET_CHUNK

    @pl.when(c == 0)
    def _():
        state[...] = jnp.zeros_like(state)

    jr = lax.broadcasted_iota(jnp.int32, (C, C), 0)
    jc = lax.broadcasted_iota(jnp.int32, (C, C), 1)
    rel = (jr - jc).astype(F32)
    j1 = lax.broadcasted_iota(jnp.int32, (C, 1), 0).astype(F32)
    for h in range(RET_HEADS):
        lg = math.log(1.0 - 2.0 ** (-5.0 - h))
        inner = jnp.where(rel >= 0, jnp.exp(lg * jnp.maximum(rel, 0.0)), 0.0)
        xi = jnp.exp(lg * (j1 + 1.0))
        zeta = jnp.exp(lg * (C - 1.0 - j1))
        decay_c = math.exp(lg * C)
        q = q_ref[0, :, h * RET_DK:(h + 1) * RET_DK]
        k = k_ref[0, :, h * RET_DK:(h + 1) * RET_DK].astype(F32) * (RET_DK ** -0.5)
        v = v_ref[0, :, h * RET_DV:(h + 1) * RET_DV]
        st = state[h]
        att = _dot_nt(q, k.astype(BF16)) * inner
        y = _dot(att.astype(BF16), v) + _dot((q.astype(F32) * xi).astype(BF16), st.astype(BF16))
        state[h] = decay_c * st + _dot_tn((k * zeta).astype(BF16), v)
        ms = jnp.mean(y * y, axis=-1, keepdims=True)
        y = y * lax.rsqrt(ms + EPS) * og_ref[...]
        gate = g_ref[0, :, h * RET_DV:(h + 1) * RET_DV].astype(F32)
        o_ref[0, :, h * RET_DV:(h + 1) * RET_DV] = (jax.nn.silu(gate) * y).astype(BF16)


def retention(z, og):
    B, S, _ = z.shape
    nck = S // RET_CHUNK
    wk = RET_HEADS * RET_DK
    wv = RET_HEADS * RET_DV
    return pl.pallas_call(
        _retention_body,
        grid=(B, nck),
        in_specs=[pl.BlockSpec((1, RET_CHUNK, wk), lambda b, c: (b, c, 0)),
                  pl.BlockSpec((1, RET_CHUNK, wk), lambda b, c: (b, c, 1)),
                  pl.BlockSpec((1, RET_CHUNK, wv), lambda b, c: (b, c, 1)),
                  pl.BlockSpec((1, RET_CHUNK, wv), lambda b, c: (b, c, 2)),
                  pl.BlockSpec((1, RET_DV), lambda b, c: (0, 0))],
        out_specs=pl.BlockSpec((1, RET_CHUNK, wv), lambda b, c: (b, c, 0)),
        out_shape=jax.ShapeDtypeStruct((B, S, wv), BF16),
        scratch_shapes=[pltpu.VMEM((RET_HEADS, RET_DK, RET_DV), F32)],
        compiler_params=_cparams(2),
        name="retention",
    )(z, z, z, z, og.reshape(1, RET_DV))


def _router_body(h_ref, g_ref, wh_ref, wl_ref, b_ref, tri_ref, e_ref, p_ref, cnt_ref, base):
    @pl.when(pl.program_id(0) == 0)
    def _():
        base[...] = jnp.zeros_like(base)

    x = h_ref[...]
    ms = jnp.mean(x * x, axis=-1, keepdims=True)
    xn = x * lax.rsqrt(ms + EPS) * g_ref[...]
    xh, xl = _split_bf16(xn)
    logits = _dot(xh, wh_ref[...]) + _dot(xh, wl_ref[...]) + _dot(xl, wh_ref[...]) + b_ref[...]
    lane = lax.broadcasted_iota(jnp.int32, logits.shape, 1)
    vals, idxs = [], []
    for _ in range(TOP_K):
        m = jnp.max(logits, axis=-1, keepdims=True)
        idx = jnp.min(jnp.where(logits == m, lane, LANES), axis=-1, keepdims=True)
        vals.append(m)
        idxs.append(idx)
        logits = jnp.where(lane == idx, -jnp.inf, logits)
    es = [jnp.exp(v - vals[0]) for v in vals]
    tot = es[0] + es[1] + es[2] + es[3]
    onehot = jnp.zeros(lane.shape, F32)
    for k in range(TOP_K):
        onehot = onehot + (lane == idxs[k]).astype(F32)
    ahead = _dot(tri_ref[...], onehot.astype(BF16)) + base[...]
    e_out = jnp.zeros(lane.shape, jnp.int32)
    p_out = jnp.zeros(lane.shape, F32)
    for k in range(TOP_K):
        rank_k = jnp.sum(jnp.where(lane == idxs[k], ahead, 0.0), axis=-1, keepdims=True)
        e_out = jnp.where(lane == k, idxs[k], e_out)
        e_out = jnp.where(lane == TOP_K + k, rank_k.astype(jnp.int32), e_out)
        p_out = jnp.where(lane == k, es[k] / tot, p_out)
    e_ref[...] = e_out
    p_ref[...] = p_out
    base[...] = base[...] + jnp.sum(onehot, axis=0, keepdims=True)
    cnt_ref[...] = base[...].astype(jnp.int32)


def moe_router(h, g, wh, wl, b, tri):
    T, D = h.shape
    tm = ROUTE_TM
    return pl.pallas_call(
        _router_body,
        grid=(T // tm,),
        in_specs=[pl.BlockSpec((tm, D), lambda i: (i, 0)),
                  pl.BlockSpec((1, D), lambda i: (0, 0)),
                  pl.BlockSpec((D, LANES), lambda i: (0, 0)),
                  pl.BlockSpec((D, LANES), lambda i: (0, 0)),
                  pl.BlockSpec((1, LANES), lambda i: (0, 0)),
                  pl.BlockSpec((tm, tm), lambda i: (0, 0))],
        out_specs=[pl.BlockSpec((tm, LANES), lambda i: (i, 0)),
                   pl.BlockSpec((tm, LANES), lambda i: (i, 0)),
                   pl.BlockSpec((1, LANES), lambda i: (0, 0))],
        out_shape=[jax.ShapeDtypeStruct((T, LANES), jnp.int32),
                   jax.ShapeDtypeStruct((T, LANES), F32),
                   jax.ShapeDtypeStruct((1, LANES), jnp.int32)],
        scratch_shapes=[pltpu.VMEM((1, LANES), F32)],
        compiler_params=_cparams(1),
        name="moe_router",
    )(h, g.reshape(1, D), wh, wl, b, tri)


def _row_copy(idx_ref, src_hbm, dst, sem, r):
    return pltpu.make_async_copy(src_hbm.at[pl.ds(idx_ref[0, 0, r], 1), :],
                                 dst.at[pl.ds(r, 1), :], sem)


def _start_row_gather(idx_ref, src_hbm, dst, sem, n):
    def start(r, carry):
        _row_copy(idx_ref, src_hbm, dst, sem, r).start()
        return carry
    lax.fori_loop(0, n, start, 0, unroll=DMA_UNROLL)


def _wait_row_gather(idx_ref, src_hbm, dst, sem, n):
    def wait(r, carry):
        _row_copy(idx_ref, src_hbm, dst, sem, r).wait()
        return carry
    lax.fori_loop(0, n, wait, 0, unroll=DMA_UNROLL)


def _moe_expert_body(blk_e_ref, n_used_ref, tok_ref, tok_next_ref, h_hbm, g_ref,
                     wgu_ref, bgu_ref, wd_ref, bd_ref, o_ref, xbuf, sem):
    del blk_e_ref
    i = pl.program_id(0)
    n_used = n_used_ref[0]
    tm = xbuf.shape[1]
    slot = lax.rem(i, 2)

    @pl.when(i == 0)
    def _():
        _start_row_gather(tok_ref, h_hbm, xbuf.at[0], sem.at[0], tm)

    @pl.when(i + 1 < n_used)
    def _():
        _start_row_gather(tok_next_ref, h_hbm, xbuf.at[1 - slot], sem.at[1 - slot], tm)

    @pl.when(i < n_used)
    def _():
        _wait_row_gather(tok_ref, h_hbm, xbuf.at[slot], sem.at[slot], tm)
        x = xbuf[slot]
        ms = jnp.mean(x * x, axis=-1, keepdims=True)
        xn = (x * lax.rsqrt(ms + EPS) * g_ref[...]).astype(BF16)
        hgl = _dot(xn, wgu_ref[0]) + bgu_ref[0]
        hg = jnp.minimum(hgl[:, :D_FF], SWIGLU_LIMIT)
        hl = jnp.clip(hgl[:, D_FF:], -SWIGLU_LIMIT, SWIGLU_LIMIT)
        a = hg * jax.nn.sigmoid(SWIGLU_ALPHA * hg) * (hl + 1.0)
        o_ref[...] = _dot(a.astype(BF16), wd_ref[0]) + bd_ref[0]

    @pl.when(i >= n_used)
    def _():
        o_ref[...] = jnp.zeros_like(o_ref)


def moe_experts(blk_e, n_used, slot_tok, h, g, wgu, bgu, wd, bd):
    T, D = h.shape
    n_blocks = blk_e.shape[0]
    tm = MOE_TM
    last = n_blocks - 1
    grid_spec = pltpu.PrefetchScalarGridSpec(
        num_scalar_prefetch=2,
        grid=(n_blocks,),
        in_specs=[pl.BlockSpec((1, 1, tm), lambda i, be, nu: (i, 0, 0), memory_space=pltpu.SMEM),
                  pl.BlockSpec((1, 1, tm), lambda i, be, nu: (jnp.minimum(i + 1, last), 0, 0),
                               memory_space=pltpu.SMEM),
                  pl.BlockSpec(memory_space=pl.ANY),
                  pl.BlockSpec((1, D), lambda i, be, nu: (0, 0)),
                  pl.BlockSpec((1, D, 2 * D_FF), lambda i, be, nu: (be[i], 0, 0)),
                  pl.BlockSpec((1, 1, 2 * D_FF), lambda i, be, nu: (be[i], 0, 0)),
                  pl.BlockSpec((1, D_FF, D), lambda i, be, nu: (be[i], 0, 0)),
                  pl.BlockSpec((1, 1, D), lambda i, be, nu: (be[i], 0, 0))],
        out_specs=pl.BlockSpec((tm, D), lambda i, be, nu: (i, 0)),
        scratch_shapes=[pltpu.VMEM((2, tm, D), F32), pltpu.SemaphoreType.DMA((2,))],
    )
    tok3 = slot_tok.reshape(n_blocks, 1, tm)
    return pl.pallas_call(
        _moe_expert_body,
        grid_spec=grid_spec,
        out_shape=jax.ShapeDtypeStruct((n_blocks * tm, D), F32),
        compiler_params=_cparams(1),
        name="moe_experts",
    )(blk_e, n_used, tok3, tok3, h, g.reshape(1, D), wgu,
      bgu.reshape(N_EXPERTS, 1, 2 * D_FF), wd, bd.reshape(N_EXPERTS, 1, D))


def _combine_ple_body(slot_ref, slot_next_ref, yb_hbm, h_ref, gate_ref, p_ref, wg_ref, wp_ref,
                      o_ref, ybuf, sem):
    i = pl.program_id(0)
    n = pl.num_programs(0)
    tm = h_ref.shape[0]
    rows = TOP_K * tm
    slot = lax.rem(i, 2)

    @pl.when(i == 0)
    def _():
        _start_row_gather(slot_ref, yb_hbm, ybuf.at[0], sem.at[0], rows)

    @pl.when(i + 1 < n)
    def _():
        _start_row_gather(slot_next_ref, yb_hbm, ybuf.at[1 - slot], sem.at[1 - slot], rows)

    _wait_row_gather(slot_ref, yb_hbm, ybuf.at[slot], sem.at[slot], rows)
    gate = gate_ref[...]
    h = h_ref[...]
    for k in range(TOP_K):
        h = h + gate[:, k:k + 1] * ybuf[slot, k * tm:(k + 1) * tm, :]
    ms = jnp.mean(h * h, axis=-1, keepdims=True)
    hn = (h * lax.rsqrt(ms + EPS)).astype(BF16)
    pg = jax.nn.sigmoid(_dot(hn, wg_ref[...]))
    o_ref[...] = h + pg * _dot(p_ref[...].astype(BF16), wp_ref[...])


def combine_ple(slots, yb, h, gate, p, wg, wp):
    T, D = h.shape
    tm = CMB_TM
    nblk = T // tm
    last = nblk - 1
    return pl.pallas_call(
        _combine_ple_body,
        grid=(nblk,),
        in_specs=[pl.BlockSpec((1, 1, TOP_K * tm), lambda i: (i, 0, 0), memory_space=pltpu.SMEM),
                  pl.BlockSpec((1, 1, TOP_K * tm), lambda i: (jnp.minimum(i + 1, last), 0, 0),
                               memory_space=pltpu.SMEM),
                  pl.BlockSpec(memory_space=pl.ANY),
                  pl.BlockSpec((tm, D), lambda i: (i, 0)),
                  pl.BlockSpec((tm, LANES), lambda i: (i, 0)),
                  pl.BlockSpec((tm, PLE_DIM), lambda i: (i, 0)),
                  pl.BlockSpec((D, D), lambda i: (0, 0)),
                  pl.BlockSpec((PLE_DIM, D), lambda i: (0, 0))],
        out_specs=pl.BlockSpec((tm, D), lambda i: (i, 0)),
        out_shape=jax.ShapeDtypeStruct((T, D), F32),
        scratch_shapes=[pltpu.VMEM((2, TOP_K * tm, D), F32), pltpu.SemaphoreType.DMA((2,))],
        compiler_params=_cparams(1),
        name="combine_ple",
    )(slots, slots, yb, h, gate, p, wg, wp)


def _routing_plan(top_e, rank, counts, T):
    tm = MOE_TM
    n = T * TOP_K
    n_blocks = n // tm + N_EXPERTS
    padded = (counts + tm - 1) // tm * tm
    pad_end = jnp.cumsum(padded)
    pad_start = pad_end - padded
    start = jnp.cumsum(counts) - counts
    slot_of = pad_start[top_e] + rank
    n_used = (pad_end[-1] // tm).astype(jnp.int32).reshape(1)
    blk_first = jnp.arange(n_blocks, dtype=jnp.int32) * tm
    blk_e = jnp.minimum(jnp.sum(blk_first[:, None] >= pad_end[None, :], axis=1),
                        N_EXPERTS - 1).astype(jnp.int32)
    tok_sorted = (jnp.argsort(top_e.reshape(-1)) // TOP_K).astype(jnp.int32)
    s = jnp.arange(n_blocks * tm, dtype=jnp.int32)
    e_s = blk_e[s // tm]
    r = s - pad_start[e_s]
    src = jnp.clip(start[e_s] + r, 0, n - 1)
    slot_tok = jnp.where(r < counts[e_s], tok_sorted[src], 0).astype(jnp.int32)
    return blk_e, n_used, slot_tok, slot_of.astype(jnp.int32)


def moe_ple_layer(h, p_i, g2, router_w, router_b, wgu, bgu, wd, bd, w_proj, w_gate):
    T, D = h.shape
    rw = jnp.pad(router_w, ((0, 0), (0, LANES - N_EXPERTS)))
    rwh = rw.astype(BF16)
    rwl = (rw - rwh.astype(F32)).astype(BF16)
    rb = jnp.pad(router_b, (0, LANES - N_EXPERTS), constant_values=NEG_INF).reshape(1, LANES)
    tri = jnp.asarray(np.tril(np.ones((ROUTE_TM, ROUTE_TM), np.float32), -1), BF16)
    eo, gate, cnt = moe_router(h, g2, rwh, rwl, rb, tri)
    blk_e, n_used, slot_tok, slot_of = _routing_plan(eo[:, :TOP_K], eo[:, TOP_K:2 * TOP_K],
                                                     cnt[0, :N_EXPERTS], T)
    yb = moe_experts(blk_e, n_used, slot_tok, h, g2, wgu.astype(BF16), bgu, wd.astype(BF16), bd)
    nblk = T // CMB_TM
    slots = slot_of.reshape(nblk, CMB_TM, TOP_K).transpose(0, 2, 1).reshape(nblk, 1, TOP_K * CMB_TM)
    return combine_ple(slots, yb, h, gate, p_i, w_gate.astype(BF16), w_proj.astype(BF16))


def _even_layout():
    idx = np.full((EV_PAD,), EV_IN, np.int64)
    idx[:Z_QN] = np.arange(Z_QN)
    qn_cols = np.zeros((NS_HEADS * NS_DH,), np.int64)
    for c in range(NS_HEADS * NS_DH):
        t, half, d = c // LANES, (c % LANES) // NS_DH, c % NS_DH
        qn_cols[c] = (half * NS_R + t) * NS_DH + d
    idx[Z_QN:Z_KC] = Z_QN + qn_cols
    idx[Z_KC:Z_GL] = np.arange(Z_KC, Z_GL)
    idx[Z_GL:Z_GL + NS_HEADS * 3] = np.arange(Z_GL, EV_IN)
    eg = np.zeros((3, LANES, NS_HEADS * NS_DH), np.float32)
    for c in range(NS_HEADS * NS_DH):
        head = qn_cols[c] // NS_DH
        for br in range(3):
            eg[br, 3 * head + br, c] = 1.0
    return idx, qn_cols, eg


def _nsa_constants(S):
    nc = (S - CMP_LEN) // CMP_STRIDE + 1
    nc_pad = S // CMP_STRIDE
    nb = S // SEL_LEN
    cstart = np.arange(nc_pad) * CMP_STRIDE
    bstart = np.arange(nb) * SEL_LEN
    lo = np.maximum(cstart[:, None], bstart[None, :])
    hi = np.minimum(cstart[:, None] + CMP_LEN, bstart[None, :] + SEL_LEN)
    ov = np.maximum(hi - lo, 0).astype(np.float32) / CMP_LEN
    ov[nc:] = 0.0
    n_chunks = S // KV_CHUNK
    ex = np.zeros((n_chunks, LANES, KV_CHUNK), np.float32)
    for c in range(n_chunks):
        keys = c * KV_CHUNK + np.arange(KV_CHUNK)
        ex[c, keys // SEL_LEN, np.arange(KV_CHUNK)] = 1.0
    return ov.T.copy(), ex


def _compress_weights(pe, w1, w2):
    half = CMP_LEN // 2
    w1r = w1.reshape(2, CMP_LEN, NS_DH, CMP_HID)
    zeros = jnp.zeros((2, half, NS_DH, CMP_HID), w1.dtype)

    def expand(part):
        g0 = jnp.stack([part, zeros], axis=2)
        g1 = jnp.stack([zeros, part], axis=2)
        return jnp.concatenate([g0, g1], axis=-1).reshape(2, half * LANES, NS_KV * CMP_HID)

    wt = expand(w1r[:, :half]).astype(BF16)
    wb = expand(w1r[:, half:]).astype(BF16)
    z2 = jnp.zeros_like(w2)
    w2b = jnp.concatenate([jnp.concatenate([w2, z2], axis=-1),
                           jnp.concatenate([z2, w2], axis=-1)], axis=1).astype(BF16)
    tilepe = lambda part: jnp.tile(part[:, :, None, :], (1, 1, NS_KV, 1)).reshape(2, 1, half * LANES)
    return tilepe(pe[:, :half]), tilepe(pe[:, half:]), wt, wb, w2b


def even_mixer_layer(h, B, S, g1, w_in, w_out, da_qn_g, da_kn_g, da_lam, da_out_g,
                     ns_qn_g, ns_kn_g, ns_cmp_pe, ns_cmp_w1, ns_cmp_w2, layer):
    T, D = h.shape
    idx, qn_cols, eg = _even_layout()
    w_in_p = jnp.take(jnp.pad(w_in, ((0, 0), (0, 1))), jnp.asarray(idx), axis=1).astype(BF16)
    z = norm_matmul(h, g1, w_in_p).reshape(B, S, EV_PAD)
    lam_init = 0.8 - 0.6 * math.exp(-0.3 * layer)
    oa = diff_attention(z, da_qn_g, da_kn_g, da_lam, da_out_g, lam_init)
    n_chunk = S // CMP_STRIDE
    kc = z[:, :, Z_KC:Z_KC + LANES].reshape(B, n_chunk, CMP_STRIDE * LANES)
    vc = z[:, :, Z_VC:Z_VC + LANES].reshape(B, n_chunk, CMP_STRIDE * LANES)
    pet, peb, wt, wb, w2b = _compress_weights(ns_cmp_pe, ns_cmp_w1, ns_cmp_w2)
    kg2 = jnp.tile(ns_kn_g, 2).reshape(1, LANES)
    kcmp, vcmp = nsa_compress(kc, vc, pet, peb, wt, wb, w2b, kg2)
    ovt, ex = _nsa_constants(S)
    ob = nsa_attention(z, kcmp, vcmp, ns_qn_g, ns_kn_g, jnp.asarray(ovt, BF16),
                       jnp.asarray(ex, BF16), jnp.asarray(eg, BF16))
    wa = w_out[:DA_HEADS * 2 * DA_DH].astype(BF16)
    wb_out = jnp.take(w_out[DA_HEADS * 2 * DA_DH:], jnp.asarray(qn_cols), axis=0).astype(BF16)
    return out_proj2(h, oa.reshape(T, -1), ob.reshape(T, -1), wa, wb_out)


def odd_mixer_layer(h, B, S, g1, w_in, w_out, out_g):
    T, D = h.shape
    z = norm_matmul(h, g1, w_in.astype(BF16)).reshape(B, S, OD_IN)
    y = retention(z, out_g)
    return out_proj1(h, y.reshape(T, -1), w_out.astype(BF16))


def kernel(x, p, norm1_g, norm2_g, ev_w_in, ev_w_out, da_qn_g, da_kn_g, da_lam, da_out_g,
           ns_qn_g, ns_kn_g, ns_cmp_pe, ns_cmp_w1, ns_cmp_w2, od_w_in, od_w_out, ret_out_g,
           router_w, router_b, moe_w_gu, moe_b_gu, moe_w_down, moe_b_down, ple_w_proj, ple_w_gate):
    B, S, D = x.shape
    T = B * S
    h = x.reshape(T, D)
    for i in range(DEPTH):
        if i % 2 == 0:
            e = i // 2
            h = even_mixer_layer(h, B, S, norm1_g[i], ev_w_in[e], ev_w_out[e], da_qn_g[e], da_kn_g[e],
                                 da_lam[e], da_out_g[e], ns_qn_g[e], ns_kn_g[e], ns_cmp_pe[e],
                                 ns_cmp_w1[e], ns_cmp_w2[e], i)
        else:
            o = i // 2
            h = odd_mixer_layer(h, B, S, norm1_g[i], od_w_in[o], od_w_out[o], ret_out_g[o])
        h = moe_ple_layer(h, p[i].reshape(T, PLE_DIM), norm2_g[i], router_w[i], router_b[i],
                          moe_w_gu[i], moe_b_gu[i], moe_w_down[i], moe_b_down[i],
                          ple_w_proj[i], ple_w_gate[i])
    return h.reshape(B, S, D)
```

```python
import functools
import math

import numpy as np
import jax
import jax.numpy as jnp
from jax import lax
from jax.experimental import pallas as pl
from jax.experimental.pallas import tpu as pltpu

F32 = jnp.float32
BF16 = jnp.bfloat16

D_MODEL = 1024
DEPTH = 4
PLE_DIM = 256
QBLK = 128
NEG_INF = -1e30
EPS = 1e-6
DA_DH = 64
DA_HEADS = 4
NS_DH = 64
NS_HEADS = 8
NS_KV = 2
NS_R = NS_HEADS // NS_KV
CMP_LEN = 32
CMP_STRIDE = 16
CMP_HID = 256
SEL_LEN = 64
SEL_TOPN = 16
WINDOW = 512
FORCE_BONUS = 1e4
RET_DK = 256
RET_HEADS = 4
RET_DV = 512
RET_CHUNK = 128
N_EXPERTS = 32
TOP_K = 4
D_FF = 1024
SWIGLU_ALPHA = 1.702
SWIGLU_LIMIT = 7.0

LANES = 128
VMEM_LIMIT = 56 * 1024 * 1024

Z_Q1, Z_Q2, Z_K1, Z_K2, Z_VA, Z_QN = 0, 256, 512, 768, 1024, 1536
Z_KC, Z_VC, Z_KS, Z_VS, Z_KW, Z_VW, Z_GL = 2048, 2176, 2304, 2432, 2560, 2688, 2816
EV_IN = 2840
EV_PAD = 3072
OD_IN = 6144

KV_PREFIX = 512
LOG2E = math.log2(math.e)
ROUTE_TM = 512
MOE_TM = 512
CMB_TM = 128
DMA_UNROLL = 8


def _cparams(n_axes):
    return pltpu.CompilerParams(dimension_semantics=("arbitrary",) * n_axes,
                                vmem_limit_bytes=VMEM_LIMIT)


def _dot(a, b):
    return jnp.dot(a, b, preferred_element_type=F32)


def _dot_nt(a, b):
    return lax.dot_general(a, b, (((1,), (1,)), ((), ())), preferred_element_type=F32)


def _dot_tn(a, b):
    return lax.dot_general(a, b, (((0,), (0,)), ((), ())), preferred_element_type=F32)


def _split_bf16(x):
    hi = x.astype(BF16)
    lo = (x - hi.astype(F32)).astype(BF16)
    return hi, lo


def _half_mask():
    return lax.broadcasted_iota(jnp.int32, (1, LANES), 1) >= 64


def _pair_head_norm(x, gain):
    upper = _half_mask()
    x2 = x * x
    s_lo = jnp.sum(jnp.where(upper, 0.0, x2), axis=-1, keepdims=True)
    s_hi = jnp.sum(jnp.where(upper, x2, 0.0), axis=-1, keepdims=True)
    r = jnp.where(upper, lax.rsqrt(s_hi / 64.0 + EPS), lax.rsqrt(s_lo / 64.0 + EPS))
    return x * r * gain


def _causal_prefix_variants(qb, S, fn):
    per = KV_PREFIX // QBLK
    for v in range(S // KV_PREFIX):
        @pl.when((qb >= v * per) & (qb < (v + 1) * per))
        def _():
            fn((v + 1) * KV_PREFIX)


def _norm_matmul_body(x_ref, g_ref, w_ref, o_ref, *, tn):
    x = x_ref[...]
    ms = jnp.mean(x * x, axis=-1, keepdims=True)
    xn = (x * lax.rsqrt(ms + EPS) * g_ref[...]).astype(BF16)
    for j in range(w_ref.shape[1] // tn):
        o_ref[:, j * tn:(j + 1) * tn] = _dot(xn, w_ref[:, j * tn:(j + 1) * tn]).astype(o_ref.dtype)


def norm_matmul(x, g, w, tm=512, tn=512):
    T, D = x.shape
    N = w.shape[1]
    return pl.pallas_call(
        functools.partial(_norm_matmul_body, tn=tn),
        grid=(T // tm,),
        in_specs=[pl.BlockSpec((tm, D), lambda i: (i, 0)),
                  pl.BlockSpec((1, D), lambda i: (0, 0)),
                  pl.BlockSpec((D, N), lambda i: (0, 0))],
        out_specs=pl.BlockSpec((tm, N), lambda i: (i, 0)),
        out_shape=jax.ShapeDtypeStruct((T, N), BF16),
        compiler_params=_cparams(1),
        name="norm_matmul",
    )(x, g.reshape(1, D), w)


def _out_proj2_body(h_ref, a_ref, b_ref, wa_ref, wb_ref, o_ref):
    o_ref[...] = h_ref[...] + _dot(a_ref[...], wa_ref[...]) + _dot(b_ref[...], wb_ref[...])


def out_proj2(h, a, b, wa, wb, tm=512):
    T, D = h.shape
    return pl.pallas_call(
        _out_proj2_body,
        grid=(T // tm,),
        in_specs=[pl.BlockSpec((tm, D), lambda i: (i, 0)),
                  pl.BlockSpec((tm, a.shape[1]), lambda i: (i, 0)),
                  pl.BlockSpec((tm, b.shape[1]), lambda i: (i, 0)),
                  pl.BlockSpec(wa.shape, lambda i: (0, 0)),
                  pl.BlockSpec(wb.shape, lambda i: (0, 0))],
        out_specs=pl.BlockSpec((tm, D), lambda i: (i, 0)),
        out_shape=jax.ShapeDtypeStruct((T, D), F32),
        compiler_params=_cparams(1),
        name="out_proj2",
    )(h, a, b, wa, wb)


def _out_proj1_body(h_ref, a_ref, wa_ref, o_ref):
    o_ref[...] = h_ref[...] + _dot(a_ref[...], wa_ref[...])


def out_proj1(h, a, wa, tm=512):
    T, D = h.shape
    return pl.pallas_call(
        _out_proj1_body,
        grid=(T // tm,),
        in_specs=[pl.BlockSpec((tm, D), lambda i: (i, 0)),
                  pl.BlockSpec((tm, a.shape[1]), lambda i: (i, 0)),
                  pl.BlockSpec(wa.shape, lambda i: (0, 0))],
        out_specs=pl.BlockSpec((tm, D), lambda i: (i, 0)),
        out_shape=jax.ShapeDtypeStruct((T, D), F32),
        compiler_params=_cparams(1),
        name="out_proj1",
    )(h, a, wa)


def _diff_attn_body(q1_ref, q2_ref, k1_ref, k2_ref, v_ref, qg_ref, kg_ref, lam_ref, og_ref,
                    o_ref, k1s, k2s, qst, *, S, lam_init):
    qb = pl.program_id(1)
    n_pairs = DA_HEADS // 2

    @pl.when(qb == 0)
    def _():
        for p in range(n_pairs):
            sl = slice(p * LANES, (p + 1) * LANES)
            k1s[:, sl] = _pair_head_norm(k1_ref[0, :, sl].astype(F32), kg_ref[...]).astype(BF16)
            k2s[:, sl] = _pair_head_norm(k2_ref[0, :, sl].astype(F32), kg_ref[...]).astype(BF16)

    upper = _half_mask()
    scale = DA_DH ** -0.5 * LOG2E
    for p in range(n_pairs):
        sl = slice(p * LANES, (p + 1) * LANES)
        for which, q_ref in enumerate((q1_ref, q2_ref)):
            q = _pair_head_norm(q_ref[0, :, sl].astype(F32), qg_ref[...]) * scale
            qst[p * 2 + which] = jnp.concatenate(
                [jnp.where(upper, 0.0, q), jnp.where(upper, q, 0.0)], axis=0).astype(BF16)

    q0 = qb * QBLK
    pos = q0 + lax.broadcasted_iota(jnp.int32, (QBLK, 1), 0)
    lam_p = lam_ref[...]
    lam = (jnp.exp(jnp.sum(lam_p[0:1] * lam_p[1:2], axis=-1, keepdims=True))
           - jnp.exp(jnp.sum(lam_p[2:3] * lam_p[3:4], axis=-1, keepdims=True)) + lam_init)

    def attend(L):
        krel = (lax.broadcasted_iota(jnp.int32, (1, L), 1) - q0).astype(F32)
        causal = lax.broadcasted_iota(jnp.int32, (QBLK, L), 1) <= pos
        for p in range(n_pairs):
            sl = slice(p * LANES, (p + 1) * LANES)
            bias = [jnp.where(causal, (2.0 ** (-8.0 * (2 * p + j + 1) / DA_HEADS) * LOG2E) * krel, NEG_INF)
                    for j in range(2)]
            o = [[None, None], [None, None]]
            for which, ks in enumerate((k1s, k2s)):
                s = _dot_nt(qst[p * 2 + which], ks[0:L, sl])
                for j in range(2):
                    h = 2 * p + j
                    sj = s[j * QBLK:(j + 1) * QBLK] + bias[j]
                    e = jnp.exp2(sj - jnp.max(sj, axis=-1, keepdims=True))
                    acc = _dot(e.astype(BF16), v_ref[0, 0:L, h * 2 * DA_DH:(h + 1) * 2 * DA_DH])
                    o[which][j] = acc / jnp.sum(e, axis=-1, keepdims=True)
            for j in range(2):
                h = 2 * p + j
                od = o[0][j] - lam * o[1][j]
                ms = jnp.mean(od * od, axis=-1, keepdims=True)
                od = od * lax.rsqrt(ms + EPS) * og_ref[...] * (1.0 - lam_init)
                o_ref[0, :, h * 2 * DA_DH:(h + 1) * 2 * DA_DH] = od.astype(BF16)

    _causal_prefix_variants(qb, S, attend)


def diff_attention(z, qg, kg, lam_p, og, lam_init):
    B, S, _ = z.shape
    nq = S // QBLK
    w = DA_HEADS * DA_DH
    body = functools.partial(_diff_attn_body, S=S, lam_init=lam_init)
    tile2 = lambda g: jnp.tile(g, 2).reshape(1, LANES)
    return pl.pallas_call(
        body,
        grid=(B, nq),
        in_specs=[pl.BlockSpec((1, QBLK, w), lambda b, q: (b, q, Z_Q1 // w)),
                  pl.BlockSpec((1, QBLK, w), lambda b, q: (b, q, Z_Q2 // w)),
                  pl.BlockSpec((1, S, w), lambda b, q: (b, 0, Z_K1 // w)),
                  pl.BlockSpec((1, S, w), lambda b, q: (b, 0, Z_K2 // w)),
                  pl.BlockSpec((1, S, 2 * w), lambda b, q: (b, 0, Z_VA // (2 * w))),
                  pl.BlockSpec((1, LANES), lambda b, q: (0, 0)),
                  pl.BlockSpec((1, LANES), lambda b, q: (0, 0)),
                  pl.BlockSpec((4, DA_DH), lambda b, q: (0, 0)),
                  pl.BlockSpec((1, LANES), lambda b, q: (0, 0))],
        out_specs=pl.BlockSpec((1, QBLK, 2 * w), lambda b, q: (b, q, 0)),
        out_shape=jax.ShapeDtypeStruct((B, S, 2 * w), BF16),
        scratch_shapes=[pltpu.VMEM((S, w), BF16), pltpu.VMEM((S, w), BF16),
                        pltpu.VMEM((DA_HEADS, 2 * QBLK, LANES), BF16)],
        compiler_params=_cparams(2),
        name="diff_attention",
    )(z, z, z, z, z, tile2(qg), tile2(kg), lam_p, og.reshape(1, LANES))


def _nsa_compress_body(kc_ref, vc_ref, pet_ref, peb_ref, wt_ref, wb_ref, w2_ref, kg_ref,
                       ko_ref, vo_ref):
    for kv, (c_ref, o_ref) in enumerate(((kc_ref, ko_ref), (vc_ref, vo_ref))):
        c = c_ref[0].astype(F32)
        a = _dot((c + pet_ref[kv]).astype(BF16), wt_ref[kv])
        bm = _dot((c + peb_ref[kv]).astype(BF16), wb_ref[kv])
        n = c.shape[0]
        hid = jax.nn.silu(a + pltpu.roll(bm, n - 1, 0))
        out = _dot(hid.astype(BF16), w2_ref[kv])
        if kv == 0:
            out = _pair_head_norm(out, kg_ref[...])
        o_ref[0] = out.astype(BF16)


def nsa_compress(kc, vc, pet, peb, wt, wb, w2, kg):
    B, n, wdt = kc.shape
    full = lambda a: pl.BlockSpec(a.shape, lambda b: (0,) * a.ndim)
    return pl.pallas_call(
        _nsa_compress_body,
        grid=(B,),
        in_specs=[pl.BlockSpec((1, n, wdt), lambda b: (b, 0, 0)),
                  pl.BlockSpec((1, n, wdt), lambda b: (b, 0, 0)),
                  full(pet), full(peb), full(wt), full(wb), full(w2), full(kg)],
        out_specs=[pl.BlockSpec((1, n, LANES), lambda b: (b, 0, 0)),
                   pl.BlockSpec((1, n, LANES), lambda b: (b, 0, 0))],
        out_shape=[jax.ShapeDtypeStruct((B, n, LANES), BF16)] * 2,
        compiler_params=_cparams(1),
        name="nsa_compress",
    )(kc, vc, pet, peb, wt, wb, w2, kg)


def _nsa_attn_body(q_ref, kc_ref, vc_ref, ks_ref, vs_ref, kw_ref, vw_ref, gl_ref,
                   qg_ref, kg_ref, ovt_ref, ex_ref, eg_ref,
                   o_ref, kss, kws, qst, sel_s, osel_s, *, S):
    qb = pl.program_id(1)
    nb = S // SEL_LEN
    nc_pad = kc_ref.shape[1]
    n_sel = min(SEL_TOPN, nb)
    wlen = WINDOW + QBLK
    M = NS_R * QBLK

    @pl.when(qb == 0)
    def _():
        kss[...] = _pair_head_norm(ks_ref[0].astype(F32), kg_ref[...]).astype(BF16)
        kws[...] = _pair_head_norm(kw_ref[0].astype(F32), kg_ref[...]).astype(BF16)

    upper = _half_mask()
    scale = NS_DH ** -0.5 * LOG2E
    slope2 = lambda g, t: 2.0 ** (-8.0 * (g * NS_R + t + 1) / NS_HEADS) * LOG2E
    q0 = qb * QBLK
    pos = q0 + lax.broadcasted_iota(jnp.int32, (QBLK, 1), 0)
    pos_row = q0 + lax.broadcasted_iota(jnp.int32, (1, QBLK), 1)

    qn = [_pair_head_norm(q_ref[0, :, t * LANES:(t + 1) * LANES].astype(F32), qg_ref[...]) * scale
          for t in range(NS_R)]
    for g in range(NS_KV):
        keep = upper if g == 1 else jnp.logical_not(upper)
        qst[g] = jnp.concatenate([jnp.where(keep, qn[t], 0.0) for t in range(NS_R)], axis=0).astype(BF16)

    cn = lax.broadcasted_iota(jnp.int32, (1, nc_pad), 1)
    n_cmp = (S - CMP_LEN) // CMP_STRIDE + 1
    c_valid = (cn * CMP_STRIDE + (CMP_LEN - 1) <= pos) & (cn < n_cmp)
    any_valid = (pos >= CMP_LEN - 1).astype(F32)
    jbt = lax.broadcasted_iota(jnp.int32, (nb, QBLK), 0)
    cur = pos_row // SEL_LEN
    forced = ((jbt == 0) | (jbt == cur) | (jbt == cur - 1)).astype(F32)
    future = jbt > cur
    kc = kc_ref[0]
    vc = vc_ref[0]
    o_cmp = []
    for g in range(NS_KV):
        s_c = _dot_nt(qst[g], kc)
        ps = []
        for t in range(NS_R):
            s_t = jnp.where(c_valid, s_c[t * QBLK:(t + 1) * QBLK], NEG_INF)
            e = jnp.exp2(s_t - jnp.max(s_t, axis=-1, keepdims=True))
            ps.append(e * (any_valid / jnp.sum(e, axis=-1, keepdims=True)))
        o_cmp.append(_dot(jnp.concatenate(ps, axis=0).astype(BF16), vc))
        p_hi, p_lo = _split_bf16(ps[0] + ps[1] + ps[2] + ps[3])
        imp = _dot_nt(ovt_ref[...], p_hi) + _dot_nt(ovt_ref[...], p_lo)
        score = jnp.where(future, NEG_INF, imp + FORCE_BONUS * forced)
        rank = jnp.zeros((nb, QBLK), jnp.int32)
        for i in range(nb):
            row_i = score[i:i + 1, :]
            ahead = (row_i > score) | ((row_i == score) & (jbt > i))
            rank = rank + ahead.astype(jnp.int32)
        sel_t = jnp.concatenate([(rank < n_sel).astype(F32),
                                 jnp.zeros((LANES - nb, QBLK), F32)], axis=0)
        sel_s[g] = sel_t.T.astype(BF16)

    def attend_selected(L):
        krel = (lax.broadcasted_iota(jnp.int32, (1, L), 1) - q0).astype(F32)
        causal = lax.broadcasted_iota(jnp.int32, (QBLK, L), 1) <= pos
        for g in range(NS_KV):
            ok = (_dot(sel_s[g], ex_ref[:, 0:L]) > 0.5) & causal
            s = _dot_nt(qst[g], kss[0:L, :])
            es, inv = [], []
            for t in range(NS_R):
                st = s[t * QBLK:(t + 1) * QBLK] + jnp.where(ok, slope2(g, t) * krel, NEG_INF)
                e = jnp.exp2(st - jnp.max(st, axis=-1, keepdims=True))
                es.append(e.astype(BF16))
                inv.append(1.0 / jnp.sum(e, axis=-1, keepdims=True))
            acc = _dot(jnp.concatenate(es, axis=0), vs_ref[0, 0:L, :])
            osel_s[g] = acc * jnp.concatenate(inv, axis=0)

    _causal_prefix_variants(qb, S, attend_selected)

    wstart = pl.multiple_of(jnp.maximum(q0 - WINDOW, 0), QBLK)
    wpos = wstart + lax.broadcasted_iota(jnp.int32, (1, wlen), 1)
    wdist = pos - wpos
    w_ok = (wdist >= 0) & (wdist < WINDOW)
    wrel = (wpos - q0).astype(F32)
    kw_win = kws[pl.ds(wstart, wlen), :]
    vw_win = vw_ref[0, pl.ds(wstart, wlen), :]
    o_win = []
    for g in range(NS_KV):
        s = _dot_nt(qst[g], kw_win)
        es, inv = [], []
        for t in range(NS_R):
            st = s[t * QBLK:(t + 1) * QBLK] + jnp.where(w_ok, slope2(g, t) * wrel, NEG_INF)
            e = jnp.exp2(st - jnp.max(st, axis=-1, keepdims=True))
            es.append(e.astype(BF16))
            inv.append(1.0 / jnp.sum(e, axis=-1, keepdims=True))
        o_win.append(_dot(jnp.concatenate(es, axis=0), vw_win) * jnp.concatenate(inv, axis=0))

    gl_hi, gl_lo = _split_bf16(gl_ref[0].astype(F32))
    gates = [jax.nn.sigmoid(_dot(gl_hi, eg_ref[c]) + _dot(gl_lo, eg_ref[c])) for c in range(3)]
    o_sel = [osel_s[g] for g in range(NS_KV)]
    for t in range(NS_R):
        sl = slice(t * LANES, (t + 1) * LANES)
        rows = slice(t * QBLK, (t + 1) * QBLK)
        pick = lambda o: jnp.where(upper, o[1][rows], o[0][rows])
        o = gates[0][:, sl] * pick(o_cmp) + gates[1][:, sl] * pick(o_sel) + gates[2][:, sl] * pick(o_win)
        o_ref[0, :, sl] = o.astype(BF16)


def nsa_attention(z, kcmp, vcmp, qg, kg, ovt, ex, eg):
    B, S, _ = z.shape
    nq = S // QBLK
    wq = NS_HEADS * NS_DH
    M = NS_R * QBLK
    body = functools.partial(_nsa_attn_body, S=S)
    zspec = lambda off: pl.BlockSpec((1, S, LANES), lambda b, q: (b, 0, off // LANES))
    full = lambda a: pl.BlockSpec(a.shape, lambda b, q: (0,) * a.ndim)
    tile2 = lambda g: jnp.tile(g, 2).reshape(1, LANES)
    qg2, kg2 = tile2(qg), tile2(kg)
    return pl.pallas_call(
        body,
        grid=(B, nq),
        in_specs=[pl.BlockSpec((1, QBLK, wq), lambda b, q: (b, q, Z_QN // wq)),
                  pl.BlockSpec((1,) + kcmp.shape[1:], lambda b, q: (b, 0, 0)),
                  pl.BlockSpec((1,) + vcmp.shape[1:], lambda b, q: (b, 0, 0)),
                  zspec(Z_KS), zspec(Z_VS), zspec(Z_KW), zspec(Z_VW),
                  pl.BlockSpec((1, QBLK, LANES), lambda b, q: (b, q, Z_GL // LANES)),
                  full(qg2), full(kg2), full(ovt), full(ex), full(eg)],
        out_specs=pl.BlockSpec((1, QBLK, wq), lambda b, q: (b, q, 0)),
        out_shape=jax.ShapeDtypeStruct((B, S, wq), BF16),
        scratch_shapes=[pltpu.VMEM((S, LANES), BF16), pltpu.VMEM((S, LANES), BF16),
                        pltpu.VMEM((NS_KV, M, LANES), BF16),
                        pltpu.VMEM((NS_KV, QBLK, LANES), BF16),
                        pltpu.VMEM((NS_KV, M, LANES), F32)],
        compiler_params=_cparams(2),
        name="nsa_attention",
    )(z, kcmp, vcmp, z, z, z, z, z, qg2, kg2, ovt, ex, eg)


def _retention_body(q_ref, k_ref, v_ref, g_ref, og_ref, o_ref, state):
    c = pl.program_id(1)
    C = RET_CHUNK

    @pl.when(c == 0)
    def _():
        state[...] = jnp.zeros_like(state)

    jr = lax.broadcasted_iota(jnp.int32, (C, C), 0)
    jc = lax.broadcasted_iota(jnp.int32, (C, C), 1)
    rel = (jr - jc).astype(F32)
    j1 = lax.broadcasted_iota(jnp.int32, (C, 1), 0).astype(F32)
    for h in range(RET_HEADS):
        lg = math.log(1.0 - 2.0 ** (-5.0 - h))
        inner = jnp.where(rel >= 0, jnp.exp(lg * jnp.maximum(rel, 0.0)), 0.0)
        xi = jnp.exp(lg * (j1 + 1.0))
        zeta = jnp.exp(lg * (C - 1.0 - j1))
        decay_c = math.exp(lg * C)
        q = q_ref[0, :, h * RET_DK:(h + 1) * RET_DK]
        k = k_ref[0, :, h * RET_DK:(h + 1) * RET_DK].astype(F32) * (RET_DK ** -0.5)
        v = v_ref[0, :, h * RET_DV:(h + 1) * RET_DV]
        st = state[h]
        att = _dot_nt(q, k.astype(BF16)) * inner
        y = _dot(att.astype(BF16), v) + _dot((q.astype(F32) * xi).astype(BF16), st.astype(BF16))
        state[h] = decay_c * st + _dot_tn((k * zeta).astype(BF16), v)
        ms = jnp.mean(y * y, axis=-1, keepdims=True)
        y = y * lax.rsqrt(ms + EPS) * og_ref[...]
        gate = g_ref[0, :, h * RET_DV:(h + 1) * RET_DV].astype(F32)
        o_ref[0, :, h * RET_DV:(h + 1) * RET_DV] = (jax.nn.silu(gate) * y).astype(BF16)


def retention(z, og):
    B, S, _ = z.shape
    nck = S // RET_CHUNK
    wk = RET_HEADS * RET_DK
    wv = RET_HEADS * RET_DV
    return pl.pallas_call(
        _retention_body,
        grid=(B, nck),
        in_specs=[pl.BlockSpec((1, RET_CHUNK, wk), lambda b, c: (b, c, 0)),
                  pl.BlockSpec((1, RET_CHUNK, wk), lambda b, c: (b, c, 1)),
                  pl.BlockSpec((1, RET_CHUNK, wv), lambda b, c: (b, c, 1)),
                  pl.BlockSpec((1, RET_CHUNK, wv), lambda b, c: (b, c, 2)),
                  pl.BlockSpec((1, RET_DV), lambda b, c: (0, 0))],
        out_specs=pl.BlockSpec((1, RET_CHUNK, wv), lambda b, c: (b, c, 0)),
        out_shape=jax.ShapeDtypeStruct((B, S, wv), BF16),
        scratch_shapes=[pltpu.VMEM((RET_HEADS, RET_DK, RET_DV), F32)],
        compiler_params=_cparams(2),
        name="retention",
    )(z, z, z, z, og.reshape(1, RET_DV))


def _router_body(h_ref, g_ref, wh_ref, wl_ref, b_ref, tri_ref, e_ref, p_ref, cnt_ref, base):
    @pl.when(pl.program_id(0) == 0)
    def _():
        base[...] = jnp.zeros_like(base)

    x = h_ref[...]
    ms = jnp.mean(x * x, axis=-1, keepdims=True)
    xn = x * lax.rsqrt(ms + EPS) * g_ref[...]
    xh, xl = _split_bf16(xn)
    logits = _dot(xh, wh_ref[...]) + _dot(xh, wl_ref[...]) + _dot(xl, wh_ref[...]) + b_ref[...]
    lane = lax.broadcasted_iota(jnp.int32, logits.shape, 1)
    vals, idxs = [], []
    for _ in range(TOP_K):
        m = jnp.max(logits, axis=-1, keepdims=True)
        idx = jnp.min(jnp.where(logits == m, lane, LANES), axis=-1, keepdims=True)
        vals.append(m)
        idxs.append(idx)
        logits = jnp.where(lane == idx, -jnp.inf, logits)
    es = [jnp.exp(v - vals[0]) for v in vals]
    tot = es[0] + es[1] + es[2] + es[3]
    onehot = jnp.zeros(lane.shape, F32)
    for k in range(TOP_K):
        onehot = onehot + (lane == idxs[k]).astype(F32)
    ahead = _dot(tri_ref[...], onehot.astype(BF16)) + base[...]
    e_out = jnp.zeros(lane.shape, jnp.int32)
    p_out = jnp.zeros(lane.shape, F32)
    for k in range(TOP_K):
        rank_k = jnp.sum(jnp.where(lane == idxs[k], ahead, 0.0), axis=-1, keepdims=True)
        e_out = jnp.where(lane == k, idxs[k], e_out)
        e_out = jnp.where(lane == TOP_K + k, rank_k.astype(jnp.int32), e_out)
        p_out = jnp.where(lane == k, es[k] / tot, p_out)
    e_ref[...] = e_out
    p_ref[...] = p_out
    base[...] = base[...] + jnp.sum(onehot, axis=0, keepdims=True)
    cnt_ref[...] = base[...].astype(jnp.int32)


def moe_router(h, g, wh, wl, b, tri):
    T, D = h.shape
    tm = ROUTE_TM
    return pl.pallas_call(
        _router_body,
        grid=(T // tm,),
        in_specs=[pl.BlockSpec((tm, D), lambda i: (i, 0)),
                  pl.BlockSpec((1, D), lambda i: (0, 0)),
                  pl.BlockSpec((D, LANES), lambda i: (0, 0)),
                  pl.BlockSpec((D, LANES), lambda i: (0, 0)),
                  pl.BlockSpec((1, LANES), lambda i: (0, 0)),
                  pl.BlockSpec((tm, tm), lambda i: (0, 0))],
        out_specs=[pl.BlockSpec((tm, LANES), lambda i: (i, 0)),
                   pl.BlockSpec((tm, LANES), lambda i: (i, 0)),
                   pl.BlockSpec((1, LANES), lambda i: (0, 0))],
        out_shape=[jax.ShapeDtypeStruct((T, LANES), jnp.int32),
                   jax.ShapeDtypeStruct((T, LANES), F32),
                   jax.ShapeDtypeStruct((1, LANES), jnp.int32)],
        scratch_shapes=[pltpu.VMEM((1, LANES), F32)],
        compiler_params=_cparams(1),
        name="moe_router",
    )(h, g.reshape(1, D), wh, wl, b, tri)


def _row_copy(idx_ref, src_hbm, dst, sem, r):
    return pltpu.make_async_copy(src_hbm.at[pl.ds(idx_ref[0, 0, r], 1), :],
                                 dst.at[pl.ds(r, 1), :], sem)


def _start_row_gather(idx_ref, src_hbm, dst, sem, n):
    for r in range(n):
        _row_copy(idx_ref, src_hbm, dst, sem, r).start(priority=r % 2)


def _start_next_row_gather(nxt, limit, idx_ref, src_hbm, buf, sem, n):
    for s in range(2):
        @pl.when((nxt < limit) & (lax.rem(nxt, 2) == s))
        def _():
            _start_row_gather(idx_ref, src_hbm, buf.at[s], sem.at[s], n)


def _wait_row_gather(idx_ref, src_hbm, dst, sem, n):
    def wait(r, carry):
        _row_copy(idx_ref, src_hbm, dst, sem, r).wait()
        return carry
    lax.fori_loop(0, n, wait, 0, unroll=DMA_UNROLL)


def _moe_expert_body(blk_e_ref, n_used_ref, tok_ref, tok_next_ref, h_hbm, g_ref,
                     wgu_ref, bgu_ref, wd_ref, bd_ref, o_ref, xbuf, sem):
    del blk_e_ref
    i = pl.program_id(0)
    n_used = n_used_ref[0]
    tm = xbuf.shape[1]
    slot = lax.rem(i, 2)

    @pl.when(i == 0)
    def _():
        _start_row_gather(tok_ref, h_hbm, xbuf.at[0], sem.at[0], tm)

    _start_next_row_gather(i + 1, n_used, tok_next_ref, h_hbm, xbuf, sem, tm)

    @pl.when(i < n_used)
    def _():
        _wait_row_gather(tok_ref, h_hbm, xbuf.at[slot], sem.at[slot], tm)
        x = xbuf[slot]
        ms = jnp.mean(x * x, axis=-1, keepdims=True)
        xn = (x * lax.rsqrt(ms + EPS) * g_ref[...]).astype(BF16)
        hgl = _dot(xn, wgu_ref[0]) + bgu_ref[0]
        hg = jnp.minimum(hgl[:, :D_FF], SWIGLU_LIMIT)
        hl = jnp.clip(hgl[:, D_FF:], -SWIGLU_LIMIT, SWIGLU_LIMIT)
        a = hg * jax.nn.sigmoid(SWIGLU_ALPHA * hg) * (hl + 1.0)
        o_ref[...] = _dot(a.astype(BF16), wd_ref[0]) + bd_ref[0]

    @pl.when(i >= n_used)
    def _():
        o_ref[...] = jnp.zeros_like(o_ref)


def moe_experts(blk_e, n_used, slot_tok, h, g, wgu, bgu, wd, bd):
    T, D = h.shape
    n_blocks = blk_e.shape[0]
    tm = MOE_TM
    last = n_blocks - 1
    grid_spec = pltpu.PrefetchScalarGridSpec(
        num_scalar_prefetch=2,
        grid=(n_blocks,),
        in_specs=[pl.BlockSpec((1, 1, tm), lambda i, be, nu: (i, 0, 0), memory_space=pltpu.SMEM),
                  pl.BlockSpec((1, 1, tm), lambda i, be, nu: (jnp.minimum(i + 1, last), 0, 0),
                               memory_space=pltpu.SMEM),
                  pl.BlockSpec(memory_space=pl.ANY),
                  pl.BlockSpec((1, D), lambda i, be, nu: (0, 0)),
                  pl.BlockSpec((1, D, 2 * D_FF), lambda i, be, nu: (be[i], 0, 0)),
                  pl.BlockSpec((1, 1, 2 * D_FF), lambda i, be, nu: (be[i], 0, 0)),
                  pl.BlockSpec((1, D_FF, D), lambda i, be, nu: (be[i], 0, 0)),
                  pl.BlockSpec((1, 1, D), lambda i, be, nu: (be[i], 0, 0))],
        out_specs=pl.BlockSpec((tm, D), lambda i, be, nu: (i, 0)),
        scratch_shapes=[pltpu.VMEM((2, tm, D), F32), pltpu.SemaphoreType.DMA((2,))],
    )
    tok3 = slot_tok.reshape(n_blocks, 1, tm)
    return pl.pallas_call(
        _moe_expert_body,
        grid_spec=grid_spec,
        out_shape=jax.ShapeDtypeStruct((n_blocks * tm, D), F32),
        compiler_params=_cparams(1),
        name="moe_experts",
    )(blk_e, n_used, tok3, tok3, h, g.reshape(1, D), wgu,
      bgu.reshape(N_EXPERTS, 1, 2 * D_FF), wd, bd.reshape(N_EXPERTS, 1, D))


def _combine_ple_body(slot_ref, slot_next_ref, yb_hbm, h_ref, gate_ref, p_ref, wg_ref, wp_ref,
                      o_ref, ybuf, sem):
    i = pl.program_id(0)
    n = pl.num_programs(0)
    tm = h_ref.shape[0]
    rows = TOP_K * tm
    slot = lax.rem(i, 2)

    @pl.when(i == 0)
    def _():
        _start_row_gather(slot_ref, yb_hbm, ybuf.at[0], sem.at[0], rows)

    _start_next_row_gather(i + 1, n, slot_next_ref, yb_hbm, ybuf, sem, rows)

    _wait_row_gather(slot_ref, yb_hbm, ybuf.at[slot], sem.at[slot], rows)
    gate = gate_ref[...]
    h = h_ref[...]
    for k in range(TOP_K):
        h = h + gate[:, k:k + 1] * ybuf[slot, k * tm:(k + 1) * tm, :]
    ms = jnp.mean(h * h, axis=-1, keepdims=True)
    hn = (h * lax.rsqrt(ms + EPS)).astype(BF16)
    pg = jax.nn.sigmoid(_dot(hn, wg_ref[...]))
    o_ref[...] = h + pg * _dot(p_ref[...].astype(BF16), wp_ref[...])


def combine_ple(slots, yb, h, gate, p, wg, wp):
    T, D = h.shape
    tm = CMB_TM
    nblk = T // tm
    last = nblk - 1
    return pl.pallas_call(
        _combine_ple_body,
        grid=(nblk,),
        in_specs=[pl.BlockSpec((1, 1, TOP_K * tm), lambda i: (i, 0, 0), memory_space=pltpu.SMEM),
                  pl.BlockSpec((1, 1, TOP_K * tm), lambda i: (jnp.minimum(i + 1, last), 0, 0),
                               memory_space=pltpu.SMEM),
                  pl.BlockSpec(memory_space=pl.ANY),
                  pl.BlockSpec((tm, D), lambda i: (i, 0)),
                  pl.BlockSpec((tm, LANES), lambda i: (i, 0)),
                  pl.BlockSpec((tm, PLE_DIM), lambda i: (i, 0)),
                  pl.BlockSpec((D, D), lambda i: (0, 0)),
                  pl.BlockSpec((PLE_DIM, D), lambda i: (0, 0))],
        out_specs=pl.BlockSpec((tm, D), lambda i: (i, 0)),
        out_shape=jax.ShapeDtypeStruct((T, D), F32),
        scratch_shapes=[pltpu.VMEM((2, TOP_K * tm, D), F32), pltpu.SemaphoreType.DMA((2,))],
        compiler_params=_cparams(1),
        name="combine_ple",
    )(slots, slots, yb, h, gate, p, wg, wp)


def _routing_plan(top_e, rank, counts, T):
    tm = MOE_TM
    n = T * TOP_K
    n_blocks = n // tm + N_EXPERTS
    padded = (counts + tm - 1) // tm * tm
    pad_end = jnp.cumsum(padded)
    pad_start = pad_end - padded
    start = jnp.cumsum(counts) - counts
    slot_of = pad_start[top_e] + rank
    n_used = (pad_end[-1] // tm).astype(jnp.int32).reshape(1)
    blk_first = jnp.arange(n_blocks, dtype=jnp.int32) * tm
    blk_e = jnp.minimum(jnp.sum(blk_first[:, None] >= pad_end[None, :], axis=1),
                        N_EXPERTS - 1).astype(jnp.int32)
    tok_sorted = (jnp.argsort(top_e.reshape(-1)) // TOP_K).astype(jnp.int32)
    tok_list = jnp.concatenate([tok_sorted, jnp.zeros((tm,), jnp.int32)])
    offs = jnp.clip(blk_first - (pad_start - start)[blk_e], 0, n)
    slot_tok = jax.vmap(lambda o: lax.dynamic_slice(tok_list, (o,), (tm,)))(offs)
    return blk_e, n_used, slot_tok, slot_of.astype(jnp.int32)


def moe_ple_layer(h, p_i, g2, router_w, router_b, wgu, bgu, wd, bd, w_proj, w_gate):
    T, D = h.shape
    rw = jnp.pad(router_w, ((0, 0), (0, LANES - N_EXPERTS)))
    rwh = rw.astype(BF16)
    rwl = (rw - rwh.astype(F32)).astype(BF16)
    rb = jnp.pad(router_b, (0, LANES - N_EXPERTS), constant_values=NEG_INF).reshape(1, LANES)
    tri = jnp.asarray(np.tril(np.ones((ROUTE_TM, ROUTE_TM), np.float32), -1), BF16)
    eo, gate, cnt = moe_router(h, g2, rwh, rwl, rb, tri)
    blk_e, n_used, slot_tok, slot_of = _routing_plan(eo[:, :TOP_K], eo[:, TOP_K:2 * TOP_K],
                                                     cnt[0, :N_EXPERTS], T)
    yb = moe_experts(blk_e, n_used, slot_tok, h, g2, wgu.astype(BF16), bgu, wd.astype(BF16), bd)
    nblk = T // CMB_TM
    slots = slot_of.reshape(nblk, CMB_TM, TOP_K).transpose(0, 2, 1).reshape(nblk, 1, TOP_K * CMB_TM)
    return combine_ple(slots, yb, h, gate, p_i, w_gate.astype(BF16), w_proj.astype(BF16))


def _even_layout():
    idx = np.full((EV_PAD,), EV_IN, np.int64)
    idx[:Z_QN] = np.arange(Z_QN)
    qn_cols = np.zeros((NS_HEADS * NS_DH,), np.int64)
    for c in range(NS_HEADS * NS_DH):
        t, half, d = c // LANES, (c % LANES) // NS_DH, c % NS_DH
        qn_cols[c] = (half * NS_R + t) * NS_DH + d
    idx[Z_QN:Z_KC] = Z_QN + qn_cols
    idx[Z_KC:Z_GL] = np.arange(Z_KC, Z_GL)
    idx[Z_GL:Z_GL + NS_HEADS * 3] = np.arange(Z_GL, EV_IN)
    eg = np.zeros((3, LANES, NS_HEADS * NS_DH), np.float32)
    for c in range(NS_HEADS * NS_DH):
        head = qn_cols[c] // NS_DH
        for br in range(3):
            eg[br, 3 * head + br, c] = 1.0
    return idx, qn_cols, eg


def _nsa_constants(S):
    nc = (S - CMP_LEN) // CMP_STRIDE + 1
    nc_pad = S // CMP_STRIDE
    nb = S // SEL_LEN
    cstart = np.arange(nc_pad) * CMP_STRIDE
    bstart = np.arange(nb) * SEL_LEN
    lo = np.maximum(cstart[:, None], bstart[None, :])
    hi = np.minimum(cstart[:, None] + CMP_LEN, bstart[None, :] + SEL_LEN)
    ov = np.maximum(hi - lo, 0).astype(np.float32) / CMP_LEN
    ov[nc:] = 0.0
    ex = np.zeros((LANES, S), np.float32)
    ex[np.arange(S) // SEL_LEN, np.arange(S)] = 1.0
    return ov.T.copy(), ex


def _compress_weights(pe, w1, w2):
    half = CMP_LEN // 2
    w1r = w1.reshape(2, CMP_LEN, NS_DH, CMP_HID)
    zeros = jnp.zeros((2, half, NS_DH, CMP_HID), w1.dtype)

    def expand(part):
        g0 = jnp.stack([part, zeros], axis=2)
        g1 = jnp.stack([zeros, part], axis=2)
        return jnp.concatenate([g0, g1], axis=-1).reshape(2, half * LANES, NS_KV * CMP_HID)

    wt = expand(w1r[:, :half]).astype(BF16)
    wb = expand(w1r[:, half:]).astype(BF16)
    z2 = jnp.zeros_like(w2)
    w2b = jnp.concatenate([jnp.concatenate([w2, z2], axis=-1),
                           jnp.concatenate([z2, w2], axis=-1)], axis=1).astype(BF16)
    tilepe = lambda part: jnp.tile(part[:, :, None, :], (1, 1, NS_KV, 1)).reshape(2, 1, half * LANES)
    return tilepe(pe[:, :half]), tilepe(pe[:, half:]), wt, wb, w2b


def even_mixer_layer(h, B, S, g1, w_in, w_out, da_qn_g, da_kn_g, da_lam, da_out_g,
                     ns_qn_g, ns_kn_g, ns_cmp_pe, ns_cmp_w1, ns_cmp_w2, layer):
    T, D = h.shape
    idx, qn_cols, eg = _even_layout()
    w_in_p = jnp.take(jnp.pad(w_in, ((0, 0), (0, 1))), jnp.asarray(idx), axis=1).astype(BF16)
    z = norm_matmul(h, g1, w_in_p).reshape(B, S, EV_PAD)
    lam_init = 0.8 - 0.6 * math.exp(-0.3 * layer)
    oa = diff_attention(z, da_qn_g, da_kn_g, da_lam, da_out_g, lam_init)
    n_chunk = S // CMP_STRIDE
    kc = z[:, :, Z_KC:Z_KC + LANES].reshape(B, n_chunk, CMP_STRIDE * LANES)
    vc = z[:, :, Z_VC:Z_VC + LANES].reshape(B, n_chunk, CMP_STRIDE * LANES)
    pet, peb, wt, wb, w2b = _compress_weights(ns_cmp_pe, ns_cmp_w1, ns_cmp_w2)
    kg2 = jnp.tile(ns_kn_g, 2).reshape(1, LANES)
    kcmp, vcmp = nsa_compress(kc, vc, pet, peb, wt, wb, w2b, kg2)
    ovt, ex = _nsa_constants(S)
    ob = nsa_attention(z, kcmp, vcmp, ns_qn_g, ns_kn_g, jnp.asarray(ovt, BF16),
                       jnp.asarray(ex, BF16), jnp.asarray(eg, BF16))
    wa = w_out[:DA_HEADS * 2 * DA_DH].astype(BF16)
    wb_out = jnp.take(w_out[DA_HEADS * 2 * DA_DH:], jnp.asarray(qn_cols), axis=0).astype(BF16)
    return out_proj2(h, oa.reshape(T, -1), ob.reshape(T, -1), wa, wb_out)


def odd_mixer_layer(h, B, S, g1, w_in, w_out, out_g):
    T, D = h.shape
    z = norm_matmul(h, g1, w_in.astype(BF16)).reshape(B, S, OD_IN)
    y = retention(z, out_g)
    return out_proj1(h, y.reshape(T, -1), w_out.astype(BF16))


def kernel(x, p, norm1_g, norm2_g, ev_w_in, ev_w_out, da_qn_g, da_kn_g, da_lam, da_out_g,
           ns_qn_g, ns_kn_g, ns_cmp_pe, ns_cmp_w1, ns_cmp_w2, od_w_in, od_w_out, ret_out_g,
           router_w, router_b, moe_w_gu, moe_b_gu, moe_w_down, moe_b_down, ple_w_proj, ple_w_gate):
    B, S, D = x.shape
    T = B * S
    h = x.reshape(T, D)
    for i in range(DEPTH):
        if i % 2 == 0:
            e = i // 2
            h = even_mixer_layer(h, B, S, norm1_g[i], ev_w_in[e], ev_w_out[e], da_qn_g[e], da_kn_g[e],
                                 da_lam[e], da_out_g[e], ns_qn_g[e], ns_kn_g[e], ns_cmp_pe[e],
                                 ns_cmp_w1[e], ns_cmp_w2[e], i)
        else:
            o = i // 2
            h = odd_mixer_layer(h, B, S, norm1_g[i], od_w_in[o], od_w_out[o], ret_out_g[o])
        h = moe_ple_layer(h, p[i].reshape(T, PLE_DIM), norm2_g[i], router_w[i], router_b[i],
                          moe_w_gu[i], moe_b_gu[i], moe_w_down[i], moe_b_down[i],
                          ple_w_proj[i], ple_w_gate[i])
    return h.reshape(B, S, D)
```
